```python
import math
import jax, jax.numpy as jnp
from jax import lax
import numpy as np

D_MODEL = 2048
BATCH = 4
SEQ = 8192
DEPTH = 1

N_HEADS_A = 16
D_LAT = 256
D_HEAD_A = 128
D_ATTN = N_HEADS_A * D_HEAD_A
N_HEADS_IDX = 16
D_IDX = 64
TOPK_MAX = 256
Q_BLOCK = 128
D_RNN = 2048
N_BLK_RNN = 16
D_BLK_RNN = D_RNN // N_BLK_RNN
CONV_W = 4
LRU_C = 8.0
N_BUCKETS = 32
MAX_DIST = 128
ALPHA = (2 * DEPTH) ** 0.25
BETA = (8 * DEPTH) ** -0.25
LN_EPS = 1e-5

IN_SIZES = (N_HEADS_A * D_LAT, D_LAT, D_ATTN, N_HEADS_IDX * D_IDX, D_IDX, N_HEADS_IDX,
            D_RNN, D_RNN, D_MODEL, D_MODEL)
D_IN = sum(IN_SIZES)
SPLIT_POINTS = tuple(int(v) for v in np.cumsum(IN_SIZES)[:-1])

kernel_name = "hybrid_dsa_rglru_gated_deepnorm"


def _layernorm(x, g, b):
    xf = x.astype(jnp.float32)
    mu = jnp.mean(xf, axis=-1, keepdims=True)
    var = jnp.mean(jnp.square(xf - mu), axis=-1, keepdims=True)
    return ((xf - mu) * lax.rsqrt(var + LN_EPS)).astype(x.dtype) * g + b


def _rmsnorm(x, g):
    xf = x.astype(jnp.float32)
    return (xf * lax.rsqrt(jnp.mean(jnp.square(xf), axis=-1, keepdims=True) + LN_EPS)).astype(x.dtype) * g


def _t5_bucket(dist):
    n = jnp.maximum(dist, 0)
    max_exact = N_BUCKETS // 2
    nf = jnp.maximum(n, 1).astype(jnp.float32)
    large = max_exact + (jnp.log(nf / max_exact) / math.log(MAX_DIST / max_exact)
                         * (N_BUCKETS - max_exact)).astype(jnp.int32)
    large = jnp.minimum(large, N_BUCKETS - 1)
    return jnp.where(n < max_exact, n, large)


def _sparse_attention(q_lat, c_kv, q_idx, k_idx, w_idx, rel_bias, w_uv):
    B, T = q_lat.shape[:2]
    K = min(TOPK_MAX, T // 4)
    nb = T // Q_BLOCK
    key_pos = jnp.arange(T, dtype=jnp.int32)
    idx_scale = (D_IDX ** -0.5) * (N_HEADS_IDX ** -0.5)
    att_scale = D_LAT ** -0.5

    def blockify(a):
        return jnp.moveaxis(a.reshape((B, nb, Q_BLOCK) + a.shape[2:]), 1, 0)

    def one_block(args):
        qb, qib, wb, blk = args
        q_pos = blk * Q_BLOCK + jnp.arange(Q_BLOCK, dtype=jnp.int32)
        s = jax.nn.relu(jnp.einsum('bqhd,bsd->bqhs', qib, k_idx).astype(jnp.float32))
        score = jnp.einsum('bqhs,bqh->bqs', s, wb.astype(jnp.float32)) * idx_scale
        causal = key_pos[None, :] <= q_pos[:, None]
        score = jnp.where(causal[None], score, -jnp.inf)
        _, sel = lax.top_k(score, K)
        c_sel = jax.vmap(lambda c_b, i_b: c_b[i_b])(c_kv, sel)
        logits = jnp.einsum('bqhd,bqkd->bqhk', qb, c_sel).astype(jnp.float32) * att_scale
        dist = q_pos[None, :, None] - sel
        bias = rel_bias[_t5_bucket(dist)]
        logits = logits + jnp.moveaxis(bias, -1, 2).astype(jnp.float32)
        logits = jnp.where((dist >= 0)[:, :, None, :], logits, -jnp.inf)
        p = jax.nn.softmax(logits, axis=-1).astype(c_sel.dtype)
        o = jnp.einsum('bqhk,bqkd->bqhd', p, c_sel)
        o = jnp.einsum('bqhd,hde->bqhe', o, w_uv)
        return o.reshape(B, Q_BLOCK, D_ATTN)

    out = lax.map(one_block, (blockify(q_lat), blockify(q_idx), blockify(w_idx),
                              jnp.arange(nb, dtype=jnp.int32)))
    return jnp.moveaxis(out, 0, 1).reshape(B, T, D_ATTN)


def _rglru(xr, conv_w, conv_b, w_gate_a, b_gate_a, w_gate_x, b_gate_x, lru_lambda):
    B, T, _ = xr.shape
    xp = jnp.pad(xr, ((0, 0), (CONV_W - 1, 0), (0, 0)))
    xc = conv_b + sum(conv_w[k] * xp[:, k:k + T] for k in range(CONV_W))
    xb = xc.reshape(B, T, N_BLK_RNN, D_BLK_RNN)
    r = jax.nn.sigmoid(jnp.einsum('btnd,nde->btne', xb, w_gate_a).reshape(B, T, D_RNN) + b_gate_a)
    i = jax.nn.sigmoid(jnp.einsum('btnd,nde->btne', xb, w_gate_x).reshape(B, T, D_RNN) + b_gate_x)
    log_a = -LRU_C * r.astype(jnp.float32) * jax.nn.softplus(-lru_lambda.astype(jnp.float32))
    a = jnp.exp(log_a)
    mult = jnp.sqrt(-jnp.expm1(2.0 * log_a))
    mult = jnp.where(jnp.arange(T)[None, :, None] == 0, 1.0, mult)
    b = mult * (i * xc).astype(jnp.float32)

    def combine(left, right):
        a1, b1 = left
        a2, b2 = right
        return a1 * a2, a2 * b1 + b2

    _, h = lax.associative_scan(combine, (a, b), axis=1)
    return h.astype(xr.dtype)


def setup_inputs(seed: int = 0) -> dict:
    key = jax.random.key(seed)
    ks = jax.random.split(key, 18)
    f32 = jnp.float32
    nrm = lambda k, shape, s: jax.random.normal(k, shape, f32) * s
    u = jax.random.uniform(ks[9], (DEPTH, D_RNN), f32, 0.9, 0.999)
    p = u ** (1.0 / LRU_C)
    lru_lambda = jnp.log(p) - jnp.log1p(-p)
    return {
        "x": jax.random.normal(ks[0], (BATCH, SEQ, D_MODEL), f32),
        "w_in": nrm(ks[1], (DEPTH, D_MODEL, D_IN), D_MODEL ** -0.5),
        "kv_norm_g": 1.0 + nrm(ks[2], (DEPTH, D_LAT), 0.02),
        "w_uv": nrm(ks[3], (DEPTH, N_HEADS_A, D_LAT, D_HEAD_A), D_LAT ** -0.5),
        "w_branch_a": nrm(ks[4], (DEPTH, D_ATTN, D_MODEL), BETA * D_ATTN ** -0.5),
        "conv_w": nrm(ks[5], (DEPTH, CONV_W, D_RNN), CONV_W ** -0.5),
        "conv_b": nrm(ks[6], (DEPTH, D_RNN), 0.02),
        "w_gate_a": nrm(ks[7], (DEPTH, N_BLK_RNN, D_BLK_RNN, D_BLK_RNN), D_BLK_RNN ** -0.5),
        "b_gate_a": nrm(ks[8], (DEPTH, D_RNN), 0.02),
        "w_gate_x": nrm(ks[10], (DEPTH, N_BLK_RNN, D_BLK_RNN, D_BLK_RNN), D_BLK_RNN ** -0.5),
        "b_gate_x": nrm(ks[11], (DEPTH, D_RNN), 0.02),
        "lru_lambda": lru_lambda,
        "w_branch_b": nrm(ks[12], (DEPTH, D_RNN, D_MODEL), BETA * D_RNN ** -0.5),
        "rel_bias": nrm(ks[13], (N_BUCKETS, N_HEADS_A), 0.5),
        "w_out": nrm(ks[14], (DEPTH, D_MODEL, D_MODEL), BETA * D_MODEL ** -0.5),
        "ln_g": 1.0 + nrm(ks[15], (DEPTH, D_MODEL), 0.02),
        "ln_b": nrm(ks[16], (DEPTH, D_MODEL), 0.02),
    }


def reference(x, w_in, kv_norm_g, w_uv, w_branch_a, conv_w, conv_b, w_gate_a, b_gate_a,
              w_gate_x, b_gate_x, lru_lambda, w_branch_b, rel_bias, w_out, ln_g, ln_b):
    B, T, _ = x.shape
    for l in range(DEPTH):
        proj = jnp.einsum('btd,dc->btc', x, w_in[l])
        (q_lat, c_kv, attn_gate, q_idx, k_idx, w_idx,
         x_rnn, rnn_gate, g_a, g_b) = jnp.split(proj, SPLIT_POINTS, axis=-1)
        q_lat = q_lat.reshape(B, T, N_HEADS_A, D_LAT)
        c_kv = _rmsnorm(c_kv, kv_norm_g[l])
        q_idx = q_idx.reshape(B, T, N_HEADS_IDX, D_IDX)
        attn = _sparse_attention(q_lat, c_kv, q_idx, k_idx, w_idx, rel_bias, w_uv[l])
        branch_a = jnp.einsum('btc,cd->btd', attn * jax.nn.silu(attn_gate), w_branch_a[l])
        h = _rglru(x_rnn, conv_w[l], conv_b[l], w_gate_a[l], b_gate_a[l],
                   w_gate_x[l], b_gate_x[l], lru_lambda[l])
        branch_b = jnp.einsum('btc,cd->btd', h * jax.nn.silu(rnn_gate), w_branch_b[l])
        merged = jax.nn.sigmoid(g_a) * branch_a + jax.nn.sigmoid(g_b) * branch_b
        sub = jnp.einsum('btd,de->bte', merged, w_out[l])
        x = _layernorm(ALPHA * x + sub, ln_g[l], ln_b[l])
    return x
```

```python
import functools
import math

import jax
import jax.numpy as jnp
from jax import lax
from jax.experimental import pallas as pl
from jax.experimental.pallas import tpu as pltpu

N_HEADS_IDX = 16
D_IDX = 64
TOPK_MAX = 256
MAX_DIST = 128
LRU_C = 8.0
LN_EPS = 1e-5

LANES = 128
V7X_VMEM_LIMIT_BYTES = 56 * 1024 * 1024

NEG = -1e30
INT_MIN = -(2 ** 31)

F32 = jnp.float32
BF16 = jnp.bfloat16


def _params(n_grid):
    return pltpu.CompilerParams(dimension_semantics=("arbitrary",) * n_grid,
                                vmem_limit_bytes=V7X_VMEM_LIMIT_BYTES)


def _mm_kernel(x_ref, w_ref, o_ref):
    o_ref[...] = jnp.dot(x_ref[...], w_ref[...], preferred_element_type=F32).astype(o_ref.dtype)


def _matmul(x, w, tm, tn, out_dtype):
    M, K = x.shape
    N = w.shape[1]
    tm, tn = min(tm, M), min(tn, N)
    assert M % tm == 0 and N % tn == 0
    return pl.pallas_call(
        _mm_kernel, grid=(M // tm, N // tn),
        in_specs=[pl.BlockSpec((tm, K), lambda i, j: (i, 0)),
                  pl.BlockSpec((K, tn), lambda i, j: (0, j))],
        out_specs=pl.BlockSpec((tm, tn), lambda i, j: (i, j)),
        out_shape=jax.ShapeDtypeStruct((M, N), out_dtype),
        compiler_params=_params(2), name="proj_rows")(x, w)


def _mm_nt_kernel(w_ref, x_ref, o_ref):
    o_ref[0] = lax.dot_general(w_ref[...], x_ref[...], (((1,), (1,)), ((), ())),
                               preferred_element_type=F32).astype(o_ref.dtype)


def _matmul_nt(w_t, x, batch, tn, tt, out_dtype, name):
    N, K = w_t.shape
    M = x.shape[0]
    T = M // batch
    tn, tt = min(tn, N), min(tt, T)
    assert N % tn == 0 and T % tt == 0
    nt = T // tt
    return pl.pallas_call(
        _mm_nt_kernel, grid=(batch, nt, N // tn),
        in_specs=[pl.BlockSpec((tn, K), lambda b, t, n: (n, 0)),
                  pl.BlockSpec((tt, K), lambda b, t, n: (b * nt + t, 0))],
        out_specs=pl.BlockSpec((1, tn, tt), lambda b, t, n: (b, n, t)),
        out_shape=jax.ShapeDtypeStruct((batch, N, T), out_dtype),
        compiler_params=_params(3), name=name)(w_t, x)


def _kvnorm_kernel(p_ref, g_ref, c_ref, ct_ref, k_ref, *, d_lat, d_idx):
    xf = p_ref[:, :d_lat]
    c = xf * lax.rsqrt(jnp.mean(jnp.square(xf), axis=-1, keepdims=True) + LN_EPS) * g_ref[...]
    c_ref[0] = c.astype(BF16)
    ct_ref[0, 0] = c.T.astype(BF16)
    k_ref[0] = p_ref[:, d_lat:d_lat + d_idx].astype(BF16)


def _kvnorm(proj_small, g, batch, tk, d_lat, d_idx):
    M, W = proj_small.shape
    T = M // batch
    nt = T // tk
    return pl.pallas_call(
        functools.partial(_kvnorm_kernel, d_lat=d_lat, d_idx=d_idx), grid=(batch, nt),
        in_specs=[pl.BlockSpec((tk, W), lambda b, t: (b * nt + t, 0)),
                  pl.BlockSpec((1, d_lat), lambda b, t: (0, 0))],
        out_specs=[pl.BlockSpec((1, tk, d_lat), lambda b, t: (b, t, 0)),
                   pl.BlockSpec((1, 1, d_lat, tk), lambda b, t: (b, t, 0, 0)),
                   pl.BlockSpec((1, tk, d_idx), lambda b, t: (b, t, 0))],
        out_shape=[jax.ShapeDtypeStruct((batch, T, d_lat), BF16),
                   jax.ShapeDtypeStruct((batch, nt, d_lat, tk), BF16),
                   jax.ShapeDtypeStruct((batch, T, d_idx), BF16)],
        compiler_params=_params(2), name="kv_norm")(proj_small, g)


def _topk_kernel(qi_ref, w_ref, k_ref, mask_ref, keys_ref, jcut_ref, *, tq, tk, rc, topk):
    i = pl.program_id(1)
    T = k_ref.shape[1]
    hi = w_ref.shape[1]
    di = k_ref.shape[2]
    n_rows = (i + 1) * tq
    q_pos = i * tq + lax.broadcasted_iota(jnp.int32, (1, tq), 1)

    def score_chunk(j, carry):
        r0 = pl.multiple_of(j * tk, tk)
        kc = k_ref[0, pl.ds(r0, tk), :]
        acc = jnp.zeros((tk, tq), F32)
        for h in range(hi):
            s = jnp.dot(kc, qi_ref[0, h * di:(h + 1) * di, :], preferred_element_type=F32)
            acc = acc + w_ref[0, h:h + 1, :] * jnp.maximum(s, 0.0)
        bits = pltpu.bitcast(acc, jnp.int32)
        key = bits ^ ((bits >> 31) & 0x7FFFFFFF)
        k_pos = r0 + lax.broadcasted_iota(jnp.int32, (tk, tq), 0)
        keys_ref[pl.ds(r0, tk), :] = jnp.where(k_pos <= q_pos, key, INT_MIN)
        return carry

    lax.fori_loop(0, n_rows // tk, score_chunk, 0)

    def count(pred):
        def body(c, acc):
            r0 = pl.multiple_of(c * rc, rc)
            m = pred(keys_ref[pl.ds(r0, rc), :], r0).astype(jnp.int32)
            for s in range(rc // 8):
                acc = acc + m[8 * s:8 * s + 8, :]
            return acc
        acc = lax.fori_loop(0, n_rows // rc, body, jnp.zeros((8, tq), jnp.int32))
        return jnp.sum(acc, axis=0, keepdims=True)

    def bit_body(b, carry):
        prefix, cnt_ge = carry
        cand = prefix | jnp.left_shift(jnp.int32(1), 31 - b)
        cand_signed = cand ^ INT_MIN
        cnt = count(lambda blk, r0: blk >= cand_signed)
        ok = cnt >= topk
        return jnp.where(ok, cand, prefix), jnp.where(ok, cnt, cnt_ge)

    prefix, cnt_ge = lax.fori_loop(
        0, 32, bit_body, (jnp.zeros((1, tq), jnp.int32), jnp.full((1, tq), n_rows, jnp.int32)))
    thr = prefix ^ INT_MIN

    jcut_ref[...] = jnp.full(jcut_ref.shape, T, jnp.int32)

    @pl.when(jnp.max(cnt_ge) > topk)
    def _():
        need = topk - count(lambda blk, r0: blk > thr)
        nbits = (T - 1).bit_length()

        def jbody(b, jc):
            cand = jc | jnp.left_shift(jnp.int32(1), nbits - 1 - b)
            rows = lambda r0: r0 + lax.broadcasted_iota(jnp.int32, (rc, tq), 0)
            f = count(lambda blk, r0: (blk == thr) & (rows(r0) < cand))
            return jnp.where(f < need, cand, jc)

        jc = lax.fori_loop(0, nbits, jbody, jnp.zeros((1, tq), jnp.int32))
        jcut_ref[...] = jnp.broadcast_to(jc, jcut_ref.shape)

    jcut = jcut_ref[0:1, :]

    def emit(c, carry):
        r0 = pl.multiple_of(c * rc, rc)
        blk = keys_ref[pl.ds(r0, rc), :]
        rows = r0 + lax.broadcasted_iota(jnp.int32, (rc, tq), 0)
        sel = (blk > thr) | ((blk == thr) & (rows <= jcut))
        sel = sel & (rows <= q_pos)
        mask_ref[0, pl.ds(r0, rc), :] = jnp.where(sel, 0.0, NEG).astype(BF16)
        return carry

    lax.fori_loop(0, n_rows // rc, emit, 0)

    def fill(c, carry):
        r0 = pl.multiple_of(c * rc, rc)
        mask_ref[0, pl.ds(r0, rc), :] = jnp.full((rc, tq), NEG, BF16)
        return carry

    lax.fori_loop(n_rows // rc, T // rc, fill, 0)


def _topk_mask(qt, w_t, k_idx, tq, qi_row_block, topk):
    batch, hi, T = w_t.shape
    di = k_idx.shape[2]
    tk = rc = 128
    assert T % tq == 0 and tq % tk == 0
    return pl.pallas_call(
        functools.partial(_topk_kernel, tq=tq, tk=tk, rc=rc, topk=topk),
        grid=(batch, T // tq),
        in_specs=[pl.BlockSpec((1, hi * di, tq), lambda b, i: (b, qi_row_block, i)),
                  pl.BlockSpec((1, hi, tq), lambda b, i: (b, 0, i)),
                  pl.BlockSpec((1, T, di), lambda b, i: (b, 0, 0))],
        out_specs=pl.BlockSpec((1, T, tq), lambda b, i: (b, 0, i)),
        out_shape=jax.ShapeDtypeStruct((batch, T, T), BF16),
        scratch_shapes=[pltpu.VMEM((T, tq), jnp.int32), pltpu.VMEM((8, tq), jnp.int32)],
        compiler_params=_params(2), name="indexer_topk")(qt, w_t, k_idx)


def _bias_kernel(rel_ref, o_ref, *, tq, n_buckets):
    h = pl.program_id(0)
    max_exact = n_buckets // 2
    kk = lax.broadcasted_iota(jnp.int32, (tq, tq), 0)
    qq = lax.broadcasted_iota(jnp.int32, (tq, tq), 1)
    for r in range(2):
        n = jnp.maximum(r * tq + qq - kk, 0)
        nf = jnp.maximum(n, 1).astype(F32)
        large = max_exact + (jnp.log(nf / max_exact) / math.log(MAX_DIST / max_exact)
                             * (n_buckets - max_exact)).astype(jnp.int32)
        bucket = jnp.where(n < max_exact, n, jnp.minimum(large, n_buckets - 1))
        tile = jnp.zeros((tq, tq), F32)
        for k in range(n_buckets):
            tile = jnp.where(bucket == k, rel_ref[k, h], tile)
        o_ref[0, r] = tile


def _bias_tiles(rel_bias, tq):
    n_buckets, n_heads = rel_bias.shape
    return pl.pallas_call(
        functools.partial(_bias_kernel, tq=tq, n_buckets=n_buckets), grid=(n_heads,),
        in_specs=[pl.BlockSpec(memory_space=pltpu.SMEM)],
        out_specs=pl.BlockSpec((1, 2, tq, tq), lambda h: (h, 0, 0, 0)),
        out_shape=jax.ShapeDtypeStruct((n_heads, 2, tq, tq), F32),
        compiler_params=_params(1), name="rel_bias_tiles")(rel_bias)


def _attn_kernel(rel_ref, q_ref, c_ref, ct_ref, mask_ref, bias_ref, wuv_ref, gate_ref, o_ref,
                 m_ref, l_ref, acc_ref, *, tq):
    i = pl.program_id(1)
    h = pl.program_id(2)
    qt = q_ref[0]
    m_ref[...] = jnp.full(m_ref.shape, NEG, F32)
    l_ref[...] = jnp.zeros(l_ref.shape, F32)
    acc_ref[...] = jnp.zeros(acc_ref.shape, F32)

    def step(j, bias):
        r0 = pl.multiple_of(j * tq, tq)
        s = jnp.dot(c_ref[0, pl.ds(r0, tq), :], qt, preferred_element_type=F32)
        s = s + mask_ref[0, pl.ds(r0, tq), :].astype(F32)
        if bias is not None:
            s = s + bias
        m_old = m_ref[0:1, :]
        m_new = jnp.maximum(m_old, jnp.max(s, axis=0, keepdims=True))
        alpha = jnp.exp(m_old - m_new)
        p = jnp.exp(s - m_new)
        l_ref[0:1, :] = alpha * l_ref[0:1, :] + jnp.sum(p, axis=0, keepdims=True)
        m_ref[0:1, :] = m_new
        acc_ref[...] = alpha * acc_ref[...] + jnp.dot(ct_ref[0, j], p.astype(BF16),
                                                      preferred_element_type=F32)

    def far(j, carry):
        step(j, None)
        return carry

    lax.fori_loop(0, jnp.maximum(i - 1, 0), far, 0)
    m_ref[0:1, :] = m_ref[0:1, :] + rel_ref[rel_ref.shape[0] - 1, h]

    @pl.when(i >= 1)
    def _():
        step(i - 1, bias_ref[0, 1])

    step(i, bias_ref[0, 0])

    o = acc_ref[...] / l_ref[0:1, :]
    val = jnp.dot(o.T.astype(BF16), wuv_ref[0], preferred_element_type=F32)
    g = gate_ref[...]
    o_ref[...] = (val * (g * jax.nn.sigmoid(g))).astype(o_ref.dtype)


def _attention(rel_bias, qt, c, ct, mask, bias, w_uv, proj, gate_col_block, tq):
    batch, T, d_lat = c.shape
    n_heads, _, d_head = w_uv.shape
    nq = T // tq
    assert tq >= MAX_DIST and d_head % LANES == 0
    return pl.pallas_call(
        functools.partial(_attn_kernel, tq=tq), grid=(batch, nq, n_heads),
        in_specs=[pl.BlockSpec(memory_space=pltpu.SMEM),
                  pl.BlockSpec((1, d_lat, tq), lambda b, i, h: (b, h, i)),
                  pl.BlockSpec((1, T, d_lat), lambda b, i, h: (b, 0, 0)),
                  pl.BlockSpec((1, nq, d_lat, tq), lambda b, i, h: (b, 0, 0, 0)),
                  pl.BlockSpec((1, T, tq), lambda b, i, h: (b, 0, i)),
                  pl.BlockSpec((1, 2, tq, tq), lambda b, i, h: (h, 0, 0, 0)),
                  pl.BlockSpec((1, d_lat, d_head), lambda b, i, h: (h, 0, 0)),
                  pl.BlockSpec((tq, d_head), lambda b, i, h: (b * nq + i, gate_col_block + h))],
        out_specs=pl.BlockSpec((tq, d_head), lambda b, i, h: (b * nq + i, h)),
        out_shape=jax.ShapeDtypeStruct((batch * T, n_heads * d_head), BF16),
        scratch_shapes=[pltpu.VMEM((8, tq), F32), pltpu.VMEM((8, tq), F32),
                        pltpu.VMEM((d_lat, tq), F32)],
        compiler_params=_params(3), name="sparse_attention")(
            rel_bias, qt, c, ct, mask, bias, w_uv, proj)


def _rglru_kernel(x_ref, gate_ref, cw_ref, cb_ref, wg_ref, ba_ref, bx_ref, lam_ref, o_ref,
                  xbuf_ref, h_ref, *, tt, conv_w):
    t = pl.program_id(2)
    dc = x_ref.shape[1]
    blk = wg_ref.shape[1]
    pad = 8

    @pl.when(t == 0)
    def _():
        xbuf_ref[0:pad, :] = jnp.zeros((pad, dc), F32)
        h_ref[...] = jnp.zeros(h_ref.shape, F32)

    xbuf_ref[pad:pad + tt, :] = x_ref[...]
    row = lax.broadcasted_iota(jnp.int32, (tt, blk), 0)
    first = (row + t * tt) == 0

    for n in range(dc // blk):
        cs = slice(n * blk, (n + 1) * blk)
        xc = cb_ref[:, cs]
        for k in range(conv_w):
            off = pad - (conv_w - 1) + k
            xc = xc + cw_ref[k:k + 1, cs] * xbuf_ref[off:off + tt, cs]
        g = jnp.dot(xc.astype(BF16), wg_ref[n], preferred_element_type=F32)
        r_gate = jax.nn.sigmoid(g[:, :blk] + ba_ref[:, cs])
        i_gate = jax.nn.sigmoid(g[:, blk:] + bx_ref[:, cs])
        lam = lam_ref[:, cs]
        softplus_neg = jnp.maximum(-lam, 0.0) + jnp.log(1.0 + jnp.exp(-jnp.abs(lam)))
        log_a = -LRU_C * r_gate * softplus_neg
        a = jnp.exp(log_a)
        mult = jnp.where(first, 1.0, jnp.sqrt(1.0 - jnp.exp(2.0 * log_a)))
        bv = mult * (i_gate * xc)
        s = 1
        while s < tt:
            valid = row >= s
            a_sh = pltpu.roll(a, s, 0)
            b_sh = pltpu.roll(bv, s, 0)
            bv = jnp.where(valid, a * b_sh + bv, bv)
            a = jnp.where(valid, a * a_sh, a)
            s *= 2
        hh = bv + a * h_ref[0:1, cs]
        h_ref[0:1, cs] = hh[tt - 1:tt, :]
        gt = gate_ref[:, cs]
        o_ref[:, cs] = (hh * (gt * jax.nn.sigmoid(gt))).astype(o_ref.dtype)

    xbuf_ref[0:pad, :] = xbuf_ref[tt:tt + pad, :]


def _rglru(proj, x_col_block, gate_col_block, conv_w, conv_b, w_gates, b_a, b_x, lam, batch, tt, dc):
    M = proj.shape[0]
    T = M // batch
    d_rnn = conv_w.shape[1]
    cw = conv_w.shape[0]
    blk = w_gates.shape[1]
    tt, dc = min(tt, T), min(dc, d_rnn)
    assert T % tt == 0 and d_rnn % dc == 0 and dc % blk == 0 and cw - 1 <= 8 <= tt
    nt, nd = T // tt, d_rnn // dc
    vec = lambda: pl.BlockSpec((1, dc), lambda b, d, t: (0, d))
    return pl.pallas_call(
        functools.partial(_rglru_kernel, tt=tt, conv_w=cw), grid=(batch, nd, nt),
        in_specs=[pl.BlockSpec((tt, dc), lambda b, d, t: (b * nt + t, x_col_block * nd + d)),
                  pl.BlockSpec((tt, dc), lambda b, d, t: (b * nt + t, gate_col_block * nd + d)),
                  pl.BlockSpec((cw, dc), lambda b, d, t: (0, d)),
                  vec(),
                  pl.BlockSpec((dc // blk, blk, 2 * blk), lambda b, d, t: (d, 0, 0)),
                  vec(), vec(), vec()],
        out_specs=pl.BlockSpec((tt, dc), lambda b, d, t: (b * nt + t, d)),
        out_shape=jax.ShapeDtypeStruct((M, d_rnn), BF16),
        scratch_shapes=[pltpu.VMEM((tt + 8, dc), F32), pltpu.VMEM((8, dc), F32)],
        compiler_params=_params(3), name="rglru")(
            proj, proj, conv_w, conv_b, w_gates, b_a, b_x, lam)


def _merge_kernel(a_ref, hb_ref, wa_ref, wb_ref, ga_ref, gb_ref, o_ref):
    br_a = jnp.dot(a_ref[...], wa_ref[...], preferred_element_type=F32)
    br_b = jnp.dot(hb_ref[...], wb_ref[...], preferred_element_type=F32)
    o_ref[...] = (jax.nn.sigmoid(ga_ref[...]) * br_a
                  + jax.nn.sigmoid(gb_ref[...]) * br_b).astype(o_ref.dtype)


def _merge(a, hb, wa, wb, proj, ga_col_block, gb_col_block, tm, tn):
    M = a.shape[0]
    d_model = wa.shape[1]
    tm, tn = min(tm, M), min(tn, d_model)
    assert M % tm == 0 and d_model % tn == 0
    nn = d_model // tn
    return pl.pallas_call(
        _merge_kernel, grid=(M // tm, nn),
        in_specs=[pl.BlockSpec((tm, a.shape[1]), lambda i, j: (i, 0)),
                  pl.BlockSpec((tm, hb.shape[1]), lambda i, j: (i, 0)),
                  pl.BlockSpec((wa.shape[0], tn), lambda i, j: (0, j)),
                  pl.BlockSpec((wb.shape[0], tn), lambda i, j: (0, j)),
                  pl.BlockSpec((tm, tn), lambda i, j: (i, ga_col_block * nn + j)),
                  pl.BlockSpec((tm, tn), lambda i, j: (i, gb_col_block * nn + j))],
        out_specs=pl.BlockSpec((tm, tn), lambda i, j: (i, j)),
        out_shape=jax.ShapeDtypeStruct((M, d_model), BF16),
        compiler_params=_params(2), name="branch_merge")(a, hb, wa, wb, proj, proj)


def _out_kernel(m_ref, w_ref, x_ref, g_ref, b_ref, o_ref, *, alpha):
    sub = jnp.dot(m_ref[...], w_ref[...], preferred_element_type=F32)
    y = alpha * x_ref[...] + sub
    mu = jnp.mean(y, axis=-1, keepdims=True)
    var = jnp.mean(jnp.square(y - mu), axis=-1, keepdims=True)
    o_ref[...] = (y - mu) * lax.rsqrt(var + LN_EPS) * g_ref[...] + b_ref[...]


def _out_norm(merged, w_out, x, g, b, alpha, tm):
    M, d = x.shape
    tm = min(tm, M)
    assert M % tm == 0
    return pl.pallas_call(
        functools.partial(_out_kernel, alpha=alpha), grid=(M // tm,),
        in_specs=[pl.BlockSpec((tm, d), lambda i: (i, 0)),
                  pl.BlockSpec((d, d), lambda i: (0, 0)),
                  pl.BlockSpec((tm, d), lambda i: (i, 0)),
                  pl.BlockSpec((1, d), lambda i: (0, 0)),
                  pl.BlockSpec((1, d), lambda i: (0, 0))],
        out_specs=pl.BlockSpec((tm, d), lambda i: (i, 0)),
        out_shape=jax.ShapeDtypeStruct((M, d), F32),
        compiler_params=_params(1), name="out_proj_layernorm")(merged, w_out, x, g, b)


def _layer(x2, batch, w_in, kv_norm_g, w_uv, w_branch_a, conv_w, conv_b, w_gate_a, b_gate_a,
           w_gate_x, b_gate_x, lru_lambda, w_branch_b, rel_bias, w_out, ln_g, ln_b, alpha):
    M, d_model = x2.shape
    T = M // batch
    n_heads, d_lat, d_head = w_uv.shape
    d_attn = n_heads * d_head
    d_rnn = conv_w.shape[1]
    hi, di = N_HEADS_IDX, D_IDX
    sizes = (n_heads * d_lat, d_lat, d_attn, hi * di, di, hi, d_rnn, d_rnn, d_model, d_model)
    assert sum(sizes) == w_in.shape[1]
    offs = [0]
    for s in sizes:
        offs.append(offs[-1] + s)
    seg = lambda k: w_in[:, offs[k]:offs[k + 1]]
    assert d_attn == d_rnn == d_model and d_model % LANES == 0
    tq = TOPK_MAX
    topk = min(TOPK_MAX, T // 4)
    assert T % tq == 0 and (n_heads * d_lat) % (hi * di) == 0

    xb = x2.astype(BF16)
    w_rows = jnp.concatenate([seg(2), seg(6), seg(7), seg(8), seg(9)], axis=1).astype(BF16)
    proj = _matmul(xb, w_rows, 1024, 1024, F32)
    small_w = d_lat + di
    small_pad = -small_w % LANES
    w_small = jnp.pad(jnp.concatenate([seg(1), seg(4)], axis=1), ((0, 0), (0, small_pad))).astype(BF16)
    proj_small = _matmul(xb, w_small, 1024, small_w + small_pad, F32)
    att_scale = d_lat ** -0.5
    idx_scale = (di ** -0.5) * (hi ** -0.5)
    w_qt = jnp.concatenate([seg(0) * att_scale, seg(3)], axis=1).T.astype(BF16)
    qt = _matmul_nt(w_qt, xb, batch, 512, 1024, BF16, "proj_queries_t")
    w_t = _matmul_nt((seg(5) * idx_scale).T.astype(BF16), xb, batch, hi, 1024, F32, "proj_idx_w_t")

    c, ct, k_idx = _kvnorm(proj_small, kv_norm_g.reshape(1, d_lat), batch, tq, d_lat, di)
    mask = _topk_mask(qt, w_t, k_idx, tq, (n_heads * d_lat) // (hi * di), topk)
    bias = _bias_tiles(rel_bias, tq)
    a = _attention(rel_bias, qt, c, ct, mask, bias, w_uv.astype(BF16), proj, 0, tq)

    w_gates = jnp.concatenate([w_gate_a, w_gate_x], axis=2).astype(BF16)
    row = lambda v: v.reshape(1, -1)
    hb = _rglru(proj, 1, 2, conv_w, row(conv_b), w_gates, row(b_gate_a), row(b_gate_x),
                row(lru_lambda), batch, 256, 512)

    merged = _merge(a, hb, w_branch_a.astype(BF16), w_branch_b.astype(BF16), proj, 3, 4, 1024, 512)
    return _out_norm(merged, w_out.astype(BF16), x2, row(ln_g), row(ln_b), alpha, 512)


def kernel(x, w_in, kv_norm_g, w_uv, w_branch_a, conv_w, conv_b, w_gate_a, b_gate_a, w_gate_x,
           b_gate_x, lru_lambda, w_branch_b, rel_bias, w_out, ln_g, ln_b):
    batch, T, d_model = x.shape
    depth = w_in.shape[0]
    alpha = (2 * depth) ** 0.25
    x2 = x.reshape(batch * T, d_model)
    for l in range(depth):
        x2 = _layer(x2, batch, w_in[l], kv_norm_g[l], w_uv[l], w_branch_a[l], conv_w[l], conv_b[l],
                    w_gate_a[l], b_gate_a[l], w_gate_x[l], b_gate_x[l], lru_lambda[l], w_branch_b[l],
                    rel_bias, w_out[l], ln_g[l], ln_b[l], alpha)
    return x2.reshape(batch, T, d_model)
```

```python
import functools
import math

import jax
import jax.numpy as jnp
from jax import lax
from jax.experimental import pallas as pl
from jax.experimental.pallas import tpu as pltpu

N_HEADS_IDX = 16
D_IDX = 64
TOPK_MAX = 256
MAX_DIST = 128
LRU_C = 8.0
LN_EPS = 1e-5

LANES = 128
V7X_VMEM_LIMIT_BYTES = 56 * 1024 * 1024

NEG = -1e30
INT_MIN = -(2 ** 31)
LOG2E = math.log2(math.e)

F32 = jnp.float32
BF16 = jnp.bfloat16


def _params(n_grid):
    return pltpu.CompilerParams(dimension_semantics=("arbitrary",) * n_grid,
                                vmem_limit_bytes=V7X_VMEM_LIMIT_BYTES)


def _mm_kernel(x_ref, w_ref, o_ref):
    o_ref[...] = jnp.dot(x_ref[...], w_ref[...], preferred_element_type=F32).astype(o_ref.dtype)


def _matmul(x, w, tm, tn, out_dtype):
    M, K = x.shape
    N = w.shape[1]
    tm, tn = min(tm, M), min(tn, N)
    assert M % tm == 0 and N % tn == 0
    return pl.pallas_call(
        _mm_kernel, grid=(M // tm, N // tn),
        in_specs=[pl.BlockSpec((tm, K), lambda i, j: (i, 0)),
                  pl.BlockSpec((K, tn), lambda i, j: (0, j))],
        out_specs=pl.BlockSpec((tm, tn), lambda i, j: (i, j)),
        out_shape=jax.ShapeDtypeStruct((M, N), out_dtype),
        compiler_params=_params(2), name="proj_rows")(x, w)


def _mm_nt_kernel(w_ref, x_ref, o_ref):
    o_ref[0] = lax.dot_general(w_ref[...], x_ref[...], (((1,), (1,)), ((), ())),
                               preferred_element_type=F32).astype(o_ref.dtype)


def _matmul_nt(w_t, x, batch, tn, tt, out_dtype, name):
    N, K = w_t.shape
    M = x.shape[0]
    T = M // batch
    tn, tt = min(tn, N), min(tt, T)
    assert N % tn == 0 and T % tt == 0
    nt = T // tt
    return pl.pallas_call(
        _mm_nt_kernel, grid=(batch, nt, N // tn),
        in_specs=[pl.BlockSpec((tn, K), lambda b, t, n: (n, 0)),
                  pl.BlockSpec((tt, K), lambda b, t, n: (b * nt + t, 0))],
        out_specs=pl.BlockSpec((1, tn, tt), lambda b, t, n: (b, n, t)),
        out_shape=jax.ShapeDtypeStruct((batch, N, T), out_dtype),
        compiler_params=_params(3), name=name)(w_t, x)


def _kvnorm_kernel(p_ref, g_ref, c_ref, ct_ref, k_ref, *, d_lat, d_idx):
    xf = p_ref[:, :d_lat]
    c = xf * lax.rsqrt(jnp.mean(jnp.square(xf), axis=-1, keepdims=True) + LN_EPS) * g_ref[...]
    c_ref[0] = c.astype(BF16)
    ct_ref[0, 0] = c.T.astype(BF16)
    k_ref[0] = p_ref[:, d_lat:d_lat + d_idx].astype(BF16)


def _kvnorm(proj_small, g, batch, tk, d_lat, d_idx):
    M, W = proj_small.shape
    T = M // batch
    nt = T // tk
    return pl.pallas_call(
        functools.partial(_kvnorm_kernel, d_lat=d_lat, d_idx=d_idx), grid=(batch, nt),
        in_specs=[pl.BlockSpec((tk, W), lambda b, t: (b * nt + t, 0)),
                  pl.BlockSpec((1, d_lat), lambda b, t: (0, 0))],
        out_specs=[pl.BlockSpec((1, tk, d_lat), lambda b, t: (b, t, 0)),
                   pl.BlockSpec((1, 1, d_lat, tk), lambda b, t: (b, t, 0, 0)),
                   pl.BlockSpec((1, tk, d_idx), lambda b, t: (b, t, 0))],
        out_shape=[jax.ShapeDtypeStruct((batch, T, d_lat), BF16),
                   jax.ShapeDtypeStruct((batch, nt, d_lat, tk), BF16),
                   jax.ShapeDtypeStruct((batch, T, d_idx), BF16)],
        compiler_params=_params(2), name="kv_norm")(proj_small, g)


def _topk_kernel(qi_ref, w_ref, k_ref, mask_ref, keys_ref, jcut_ref, *, tq, tk, rc, topk):
    i = pl.program_id(1)
    T = k_ref.shape[1]
    hi = w_ref.shape[1]
    di = k_ref.shape[2]
    n_rows = (i + 1) * tq
    q_pos = i * tq + lax.broadcasted_iota(jnp.int32, (1, tq), 1)

    def score_chunk(j, carry):
        r0 = pl.multiple_of(j * tk, tk)
        kc = k_ref[0, pl.ds(r0, tk), :]
        acc = jnp.zeros((tk, tq), F32)
        for h in range(hi):
            s = jnp.dot(kc, qi_ref[0, h * di:(h + 1) * di, :], preferred_element_type=F32)
            acc = acc + w_ref[0, h:h + 1, :] * jnp.maximum(s, 0.0)
        bits = pltpu.bitcast(acc, jnp.int32)
        key = bits ^ ((bits >> 31) & 0x7FFFFFFF)
        k_pos = r0 + lax.broadcasted_iota(jnp.int32, (tk, tq), 0)
        keys_ref[pl.ds(r0, tk), :] = jnp.where(k_pos <= q_pos, key, INT_MIN)
        return carry

    lax.fori_loop(0, n_rows // tk, score_chunk, 0)

    def count(pred):
        def body(c, acc):
            r0 = pl.multiple_of(c * rc, rc)
            m = pred(keys_ref[pl.ds(r0, rc), :], r0).astype(jnp.int32)
            for s in range(rc // 8):
                acc = acc + m[8 * s:8 * s + 8, :]
            return acc
        acc = lax.fori_loop(0, n_rows // rc, body, jnp.zeros((8, tq), jnp.int32))
        return jnp.sum(acc, axis=0, keepdims=True)

    def bit_body(carry):
        b, _, prefix, cnt_ge = carry
        cand = prefix | jnp.left_shift(jnp.int32(1), 31 - b)
        cand_signed = cand ^ INT_MIN
        cnt = count(lambda blk, r0: blk >= cand_signed)
        ok = cnt >= topk
        cnt_ge = jnp.where(ok, cnt, cnt_ge)
        return b + 1, (jnp.max(cnt_ge) > topk).astype(jnp.int32), jnp.where(ok, cand, prefix), cnt_ge

    _, _, prefix, cnt_ge = lax.while_loop(
        lambda carry: (carry[0] < 32) & (carry[1] > 0), bit_body,
        (jnp.int32(0), (n_rows > topk).astype(jnp.int32),
         jnp.zeros((1, tq), jnp.int32), jnp.full((1, tq), n_rows, jnp.int32)))
    thr = prefix ^ INT_MIN

    jcut_ref[...] = jnp.full(jcut_ref.shape, T, jnp.int32)

    @pl.when(jnp.max(cnt_ge) > topk)
    def _():
        need = topk - count(lambda blk, r0: blk > thr)
        nbits = (T - 1).bit_length()

        def jbody(b, jc):
            cand = jc | jnp.left_shift(jnp.int32(1), nbits - 1 - b)
            rows = lambda r0: r0 + lax.broadcasted_iota(jnp.int32, (rc, tq), 0)
            f = count(lambda blk, r0: (blk == thr) & (rows(r0) < cand))
            return jnp.where(f < need, cand, jc)

        jc = lax.fori_loop(0, nbits, jbody, jnp.zeros((1, tq), jnp.int32))
        jcut_ref[...] = jnp.broadcast_to(jc, jcut_ref.shape)

    jcut = jcut_ref[0:1, :]

    def emit(c, carry):
        r0 = pl.multiple_of(c * rc, rc)
        blk = keys_ref[pl.ds(r0, rc), :]
        rows = r0 + lax.broadcasted_iota(jnp.int32, (rc, tq), 0)
        sel = (blk > thr) | ((blk == thr) & (rows <= jcut))
        sel = sel & (rows <= q_pos)
        mask_ref[0, pl.ds(r0, rc), :] = jnp.where(sel, 0.0, NEG).astype(BF16)
        return carry

    lax.fori_loop(0, n_rows // rc, emit, 0)

    def fill(c, carry):
        r0 = pl.multiple_of(c * rc, rc)
        mask_ref[0, pl.ds(r0, rc), :] = jnp.full((rc, tq), NEG, BF16)
        return carry

    lax.fori_loop(n_rows // rc, T // rc, fill, 0)


def _topk_mask(qt, w_t, k_idx, tq, qi_row_block, topk):
    batch, hi, T = w_t.shape
    di = k_idx.shape[2]
    tk, rc = 128, 256
    assert T % tq == 0 and tq % tk == 0 and tq % rc == 0
    return pl.pallas_call(
        functools.partial(_topk_kernel, tq=tq, tk=tk, rc=rc, topk=topk),
        grid=(batch, T // tq),
        in_specs=[pl.BlockSpec((1, hi * di, tq), lambda b, i: (b, qi_row_block, i)),
                  pl.BlockSpec((1, hi, tq), lambda b, i: (b, 0, i)),
                  pl.BlockSpec((1, T, di), lambda b, i: (b, 0, 0))],
        out_specs=pl.BlockSpec((1, T, tq), lambda b, i: (b, 0, i)),
        out_shape=jax.ShapeDtypeStruct((batch, T, T), BF16),
        scratch_shapes=[pltpu.VMEM((T, tq), jnp.int32), pltpu.VMEM((8, tq), jnp.int32)],
        compiler_params=_params(2), name="indexer_topk")(qt, w_t, k_idx)


def _bias_kernel(rel_ref, o_ref, *, tq, n_buckets):
    h = pl.program_id(0)
    max_exact = n_buckets // 2
    kk = lax.broadcasted_iota(jnp.int32, (tq, tq), 0)
    qq = lax.broadcasted_iota(jnp.int32, (tq, tq), 1)
    for r in range(2):
        n = jnp.maximum(r * tq + qq - kk, 0)
        nf = jnp.maximum(n, 1).astype(F32)
        large = max_exact + (jnp.log(nf / max_exact) / math.log(MAX_DIST / max_exact)
                             * (n_buckets - max_exact)).astype(jnp.int32)
        bucket = jnp.where(n < max_exact, n, jnp.minimum(large, n_buckets - 1))
        tile = jnp.zeros((tq, tq), F32)
        for k in range(n_buckets):
            tile = jnp.where(bucket == k, rel_ref[k, h], tile)
        o_ref[0, r] = tile * LOG2E


def _bias_tiles(rel_bias, tq):
    n_buckets, n_heads = rel_bias.shape
    return pl.pallas_call(
        functools.partial(_bias_kernel, tq=tq, n_buckets=n_buckets), grid=(n_heads,),
        in_specs=[pl.BlockSpec(memory_space=pltpu.SMEM)],
        out_specs=pl.BlockSpec((1, 2, tq, tq), lambda h: (h, 0, 0, 0)),
        out_shape=jax.ShapeDtypeStruct((n_heads, 2, tq, tq), F32),
        compiler_params=_params(1), name="rel_bias_tiles")(rel_bias)


def _attn_kernel(rel_ref, q_ref, c_ref, ct_ref, mask_ref, bias_ref, wuv_ref, gate_ref, o_ref,
                 m_ref, l_ref, acc_ref, *, tq, n_group):
    i = pl.program_id(1)
    g = pl.program_id(2)
    d_lat = c_ref.shape[2]
    d_head = wuv_ref.shape[2]
    m_ref[...] = jnp.full(m_ref.shape, NEG, F32)
    l_ref[...] = jnp.zeros(l_ref.shape, F32)
    acc_ref[...] = jnp.zeros(acc_ref.shape, F32)

    def step(j, bias_slot):
        r0 = pl.multiple_of(j * tq, tq)
        cj = c_ref[0, pl.ds(r0, tq), :]
        ctj = ct_ref[0, j]
        mk = mask_ref[0, pl.ds(r0, tq), :].astype(F32)
        for hh in range(n_group):
            s = jnp.dot(cj, q_ref[0, hh * d_lat:(hh + 1) * d_lat, :], preferred_element_type=F32) + mk
            if bias_slot is not None:
                s = s + bias_ref[hh, bias_slot]
            m_old = m_ref[hh, 0:1, :]
            m_new = jnp.maximum(m_old, jnp.max(s, axis=0, keepdims=True))
            alpha = jnp.exp2(m_old - m_new)
            p = jnp.exp2(s - m_new)
            l_ref[hh, 0:1, :] = alpha * l_ref[hh, 0:1, :] + jnp.sum(p, axis=0, keepdims=True)
            m_ref[hh, 0:1, :] = m_new
            acc_ref[hh] = alpha * acc_ref[hh] + jnp.dot(ctj, p.astype(BF16), preferred_element_type=F32)

    def far(j, carry):
        step(j, None)
        return carry

    lax.fori_loop(0, jnp.maximum(i - 1, 0), far, 0)
    for hh in range(n_group):
        m_ref[hh, 0:1, :] = m_ref[hh, 0:1, :] + rel_ref[rel_ref.shape[0] - 1, g * n_group + hh] * LOG2E

    def near(n, carry):
        step(i - 1 + n, 1 - n)
        return carry

    lax.fori_loop(jnp.where(i == 0, 1, 0), 2, near, 0)

    for hh in range(n_group):
        o = acc_ref[hh] / l_ref[hh, 0:1, :]
        val = jnp.dot(o.T.astype(BF16), wuv_ref[hh], preferred_element_type=F32)
        gt = gate_ref[:, hh * d_head:(hh + 1) * d_head]
        o_ref[:, hh * d_head:(hh + 1) * d_head] = (val * (gt * jax.nn.sigmoid(gt))).astype(o_ref.dtype)


def _attention(rel_bias, qt, c, ct, mask, bias, w_uv, proj, gate_col_block, tq, n_group):
    batch, T, d_lat = c.shape
    n_heads, _, d_head = w_uv.shape
    nq = T // tq
    ng = n_heads // n_group
    assert tq >= MAX_DIST and d_head % LANES == 0 and n_heads % n_group == 0
    return pl.pallas_call(
        functools.partial(_attn_kernel, tq=tq, n_group=n_group), grid=(batch, nq, ng),
        in_specs=[pl.BlockSpec(memory_space=pltpu.SMEM),
                  pl.BlockSpec((1, n_group * d_lat, tq), lambda b, i, g: (b, g, i)),
                  pl.BlockSpec((1, T, d_lat), lambda b, i, g: (b, 0, 0)),
                  pl.BlockSpec((1, nq, d_lat, tq), lambda b, i, g: (b, 0, 0, 0)),
                  pl.BlockSpec((1, T, tq), lambda b, i, g: (b, 0, i)),
                  pl.BlockSpec((n_group, 2, tq, tq), lambda b, i, g: (g, 0, 0, 0)),
                  pl.BlockSpec((n_group, d_lat, d_head), lambda b, i, g: (g, 0, 0)),
                  pl.BlockSpec((tq, n_group * d_head),
                               lambda b, i, g: (b * nq + i, gate_col_block * ng + g))],
        out_specs=pl.BlockSpec((tq, n_group * d_head), lambda b, i, g: (b * nq + i, g)),
        out_shape=jax.ShapeDtypeStruct((batch * T, n_heads * d_head), BF16),
        scratch_shapes=[pltpu.VMEM((n_group, 8, tq), F32), pltpu.VMEM((n_group, 8, tq), F32),
                        pltpu.VMEM((n_group, d_lat, tq), F32)],
        compiler_params=_params(3), name="sparse_attention")(
            rel_bias, qt, c, ct, mask, bias, w_uv, proj)


def _rglru_kernel(x_ref, gate_ref, cw_ref, cb_ref, wg_ref, ba_ref, bx_ref, lam_ref, o_ref,
                  xbuf_ref, h_ref, *, tt, conv_w):
    t = pl.program_id(2)
    dc = x_ref.shape[1]
    blk = wg_ref.shape[1]
    pad = 8

    @pl.when(t == 0)
    def _():
        xbuf_ref[0:pad, :] = jnp.zeros((pad, dc), F32)
        h_ref[...] = jnp.zeros(h_ref.shape, F32)

    xbuf_ref[pad:pad + tt, :] = x_ref[...]
    row = lax.broadcasted_iota(jnp.int32, (tt, blk), 0)
    first = (row + t * tt) == 0

    for n in range(dc // blk):
        cs = slice(n * blk, (n + 1) * blk)
        xc = cb_ref[:, cs]
        for k in range(conv_w):
            off = pad - (conv_w - 1) + k
            xc = xc + cw_ref[k:k + 1, cs] * xbuf_ref[off:off + tt, cs]
        g = jnp.dot(xc.astype(BF16), wg_ref[n], preferred_element_type=F32)
        r_gate = jax.nn.sigmoid(g[:, :blk] + ba_ref[:, cs])
        i_gate = jax.nn.sigmoid(g[:, blk:] + bx_ref[:, cs])
        lam = lam_ref[:, cs]
        softplus_neg = jnp.maximum(-lam, 0.0) + jnp.log(1.0 + jnp.exp(-jnp.abs(lam)))
        log_a = -LRU_C * r_gate * softplus_neg
        a = jnp.exp(log_a)
        mult = jnp.where(first, 1.0, jnp.sqrt(1.0 - jnp.exp(2.0 * log_a)))
        bv = mult * (i_gate * xc)
        s = 1
        while s < tt:
            valid = row >= s
            a_sh = pltpu.roll(a, s, 0)
            b_sh = pltpu.roll(bv, s, 0)
            bv = jnp.where(valid, a * b_sh + bv, bv)
            a = jnp.where(valid, a * a_sh, a)
            s *= 2
        hh = bv + a * h_ref[0:1, cs]
        h_ref[0:1, cs] = hh[tt - 1:tt, :]
        gt = gate_ref[:, cs]
        o_ref[:, cs] = (hh * (gt * jax.nn.sigmoid(gt))).astype(o_ref.dtype)

    xbuf_ref[0:pad, :] = xbuf_ref[tt:tt + pad, :]


def _rglru(proj, x_col_block, gate_col_block, conv_w, conv_b, w_gates, b_a, b_x, lam, batch, tt, dc):
    M = proj.shape[0]
    T = M // batch
    d_rnn = conv_w.shape[1]
    cw = conv_w.shape[0]
    blk = w_gates.shape[1]
    tt, dc = min(tt, T), min(dc, d_rnn)
    assert T % tt == 0 and d_rnn % dc == 0 and dc % blk == 0 and cw - 1 <= 8 <= tt
    nt, nd = T // tt, d_rnn // dc
    vec = lambda: pl.BlockSpec((1, dc), lambda b, d, t: (0, d))
    return pl.pallas_call(
        functools.partial(_rglru_kernel, tt=tt, conv_w=cw), grid=(batch, nd, nt),
        in_specs=[pl.BlockSpec((tt, dc), lambda b, d, t: (b * nt + t, x_col_block * nd + d)),
                  pl.BlockSpec((tt, dc), lambda b, d, t: (b * nt + t, gate_col_block * nd + d)),
                  pl.BlockSpec((cw, dc), lambda b, d, t: (0, d)),
                  vec(),
                  pl.BlockSpec((dc // blk, blk, 2 * blk), lambda b, d, t: (d, 0, 0)),
                  vec(), vec(), vec()],
        out_specs=pl.BlockSpec((tt, dc), lambda b, d, t: (b * nt + t, d)),
        out_shape=jax.ShapeDtypeStruct((M, d_rnn), BF16),
        scratch_shapes=[pltpu.VMEM((tt + 8, dc), F32), pltpu.VMEM((8, dc), F32)],
        compiler_params=_params(3), name="rglru")(
            proj, proj, conv_w, conv_b, w_gates, b_a, b_x, lam)


def _merge_kernel(a_ref, hb_ref, wa_ref, wb_ref, ga_ref, gb_ref, o_ref):
    br_a = jnp.dot(a_ref[...], wa_ref[...], preferred_element_type=F32)
    br_b = jnp.dot(hb_ref[...], wb_ref[...], preferred_element_type=F32)
    o_ref[...] = (jax.nn.sigmoid(ga_ref[...]) * br_a
                  + jax.nn.sigmoid(gb_ref[...]) * br_b).astype(o_ref.dtype)


def _merge(a, hb, wa, wb, proj, ga_col_block, gb_col_block, tm, tn):
    M = a.shape[0]
    d_model = wa.shape[1]
    tm, tn = min(tm, M), min(tn, d_model)
    assert M % tm == 0 and d_model % tn == 0
    nn = d_model // tn
    return pl.pallas_call(
        _merge_kernel, grid=(M // tm, nn),
        in_specs=[pl.BlockSpec((tm, a.shape[1]), lambda i, j: (i, 0)),
                  pl.BlockSpec((tm, hb.shape[1]), lambda i, j: (i, 0)),
                  pl.BlockSpec((wa.shape[0], tn), lambda i, j: (0, j)),
                  pl.BlockSpec((wb.shape[0], tn), lambda i, j: (0, j)),
                  pl.BlockSpec((tm, tn), lambda i, j: (i, ga_col_block * nn + j)),
                  pl.BlockSpec((tm, tn), lambda i, j: (i, gb_col_block * nn + j))],
        out_specs=pl.BlockSpec((tm, tn), lambda i, j: (i, j)),
        out_shape=jax.ShapeDtypeStruct((M, d_model), BF16),
        compiler_params=_params(2), name="branch_merge")(a, hb, wa, wb, proj, proj)


def _out_kernel(m_ref, w_ref, x_ref, g_ref, b_ref, o_ref, *, alpha):
    sub = jnp.dot(m_ref[...], w_ref[...], preferred_element_type=F32)
    y = alpha * x_ref[...] + sub
    mu = jnp.mean(y, axis=-1, keepdims=True)
    var = jnp.mean(jnp.square(y - mu), axis=-1, keepdims=True)
    o_ref[...] = (y - mu) * lax.rsqrt(var + LN_EPS) * g_ref[...] + b_ref[...]


def _out_norm(merged, w_out, x, g, b, alpha, tm):
    M, d = x.shape
    tm = min(tm, M)
    assert M % tm == 0
    return pl.pallas_call(
        functools.partial(_out_kernel, alpha=alpha), grid=(M // tm,),
        in_specs=[pl.BlockSpec((tm, d), lambda i: (i, 0)),
                  pl.BlockSpec((d, d), lambda i: (0, 0)),
                  pl.BlockSpec((tm, d), lambda i: (i, 0)),
                  pl.BlockSpec((1, d), lambda i: (0, 0)),
                  pl.BlockSpec((1, d), lambda i: (0, 0))],
        out_specs=pl.BlockSpec((tm, d), lambda i: (i, 0)),
        out_shape=jax.ShapeDtypeStruct((M, d), F32),
        compiler_params=_params(1), name="out_proj_layernorm")(merged, w_out, x, g, b)


def _layer(x2, batch, w_in, kv_norm_g, w_uv, w_branch_a, conv_w, conv_b, w_gate_a, b_gate_a,
           w_gate_x, b_gate_x, lru_lambda, w_branch_b, rel_bias, w_out, ln_g, ln_b, alpha):
    M, d_model = x2.shape
    T = M // batch
    n_heads, d_lat, d_head = w_uv.shape
    d_attn = n_heads * d_head
    d_rnn = conv_w.shape[1]
    hi, di = N_HEADS_IDX, D_IDX
    sizes = (n_heads * d_lat, d_lat, d_attn, hi * di, di, hi, d_rnn, d_rnn, d_model, d_model)
    assert sum(sizes) == w_in.shape[1]
    offs = [0]
    for s in sizes:
        offs.append(offs[-1] + s)
    seg = lambda k: w_in[:, offs[k]:offs[k + 1]]
    assert d_attn == d_rnn == d_model and d_model % LANES == 0
    tq = TOPK_MAX
    topk = min(TOPK_MAX, T // 4)
    assert T % tq == 0 and (n_heads * d_lat) % (hi * di) == 0

    xb = x2.astype(BF16)
    w_rows = jnp.concatenate([seg(2), seg(6), seg(7), seg(8), seg(9)], axis=1).astype(BF16)
    proj = _matmul(xb, w_rows, 1024, 1024, F32)
    small_w = d_lat + di
    small_pad = -small_w % LANES
    w_small = jnp.pad(jnp.concatenate([seg(1), seg(4)], axis=1), ((0, 0), (0, small_pad))).astype(BF16)
    proj_small = _matmul(xb, w_small, 1024, small_w + small_pad, F32)
    att_scale = d_lat ** -0.5 * LOG2E
    idx_scale = (di ** -0.5) * (hi ** -0.5)
    w_qt = jnp.concatenate([seg(0) * att_scale, seg(3)], axis=1).T.astype(BF16)
    qt = _matmul_nt(w_qt, xb, batch, 512, 1024, BF16, "proj_queries_t")
    w_t = _matmul_nt((seg(5) * idx_scale).T.astype(BF16), xb, batch, hi, 1024, F32, "proj_idx_w_t")

    c, ct, k_idx = _kvnorm(proj_small, kv_norm_g.reshape(1, d_lat), batch, tq, d_lat, di)
    mask = _topk_mask(qt, w_t, k_idx, tq, (n_heads * d_lat) // (hi * di), topk)
    bias = _bias_tiles(rel_bias, tq)
    a = _attention(rel_bias, qt, c, ct, mask, bias, w_uv.astype(BF16), proj, 0, tq, 8)

    w_gates = jnp.concatenate([w_gate_a, w_gate_x], axis=2).astype(BF16)
    row = lambda v: v.reshape(1, -1)
    hb = _rglru(proj, 1, 2, conv_w, row(conv_b), w_gates, row(b_gate_a), row(b_gate_x),
                row(lru_lambda), batch, 256, 512)

    merged = _merge(a, hb, w_branch_a.astype(BF16), w_branch_b.astype(BF16), proj, 3, 4, 1024, 512)
    return _out_norm(merged, w_out.astype(BF16), x2, row(ln_g), row(ln_b), alpha, 512)


def kernel(x, w_in, kv_norm_g, w_uv, w_branch_a, conv_w, conv_b, w_gate_a, b_gate_a, w_gate_x,
           b_gate_x, lru_lambda, w_branch_b, rel_bias, w_out, ln_g, ln_b):
    batch, T, d_model = x.shape
    depth = w_in.shape[0]
    alpha = (2 * depth) ** 0.25
    x2 = x.reshape(batch * T, d_model)
    for l in range(depth):
        x2 = _layer(x2, batch, w_in[l], kv_norm_g[l], w_uv[l], w_branch_a[l], conv_w[l], conv_b[l],
                    w_gate_a[l], b_gate_a[l], w_gate_x[l], b_gate_x[l], lru_lambda[l], w_branch_b[l],
                    rel_bias, w_out[l], ln_g[l], ln_b[l], alpha)
    return x2.reshape(batch, T, d_model)
```

```python
import functools
import math

import jax
import jax.numpy as jnp
from jax import lax
from jax.experimental import pallas as pl
from jax.experimental.pallas import tpu as pltpu

N_HEADS_IDX = 16
D_IDX = 64
TOPK_MAX = 256
MAX_DIST = 128
LRU_C = 8.0
LN_EPS = 1e-5

LANES = 128
SUBLANES = 8
V7X_VMEM_LIMIT_BYTES = 56 * 1024 * 1024

NEG = -1e30
INT_MIN = -(2 ** 31)
LOG2E = math.log2(math.e)

F32 = jnp.float32
BF16 = jnp.bfloat16


def _sigmoid(x):
    return 0.5 * jnp.tanh(0.5 * x) + 0.5


def _params(n_grid):
    return pltpu.CompilerParams(dimension_semantics=("arbitrary",) * n_grid,
                                vmem_limit_bytes=V7X_VMEM_LIMIT_BYTES)


def _mm_kernel(x_ref, w_ref, o_ref):
    o_ref[...] = jnp.dot(x_ref[...], w_ref[...], preferred_element_type=F32).astype(o_ref.dtype)


def _matmul(x, w, tm, tn, out_dtype):
    M, K = x.shape
    N = w.shape[1]
    tm, tn = min(tm, M), min(tn, N)
    assert M % tm == 0 and N % tn == 0
    return pl.pallas_call(
        _mm_kernel, grid=(M // tm, N // tn),
        in_specs=[pl.BlockSpec((tm, K), lambda i, j: (i, 0)),
                  pl.BlockSpec((K, tn), lambda i, j: (0, j))],
        out_specs=pl.BlockSpec((tm, tn), lambda i, j: (i, j)),
        out_shape=jax.ShapeDtypeStruct((M, N), out_dtype),
        compiler_params=_params(2), name="proj_rows")(x, w)


def _mm_nt_kernel(w_ref, x_ref, o_ref):
    o_ref[0] = lax.dot_general(w_ref[...], x_ref[...], (((1,), (1,)), ((), ())),
                               preferred_element_type=F32).astype(o_ref.dtype)


def _matmul_nt(w_t, x, batch, tn, tt, out_dtype, name):
    N, K = w_t.shape
    M = x.shape[0]
    T = M // batch
    tn, tt = min(tn, N), min(tt, T)
    assert N % tn == 0 and T % tt == 0
    nt = T // tt
    return pl.pallas_call(
        _mm_nt_kernel, grid=(batch, nt, N // tn),
        in_specs=[pl.BlockSpec((tn, K), lambda b, t, n: (n, 0)),
                  pl.BlockSpec((tt, K), lambda b, t, n: (b * nt + t, 0))],
        out_specs=pl.BlockSpec((1, tn, tt), lambda b, t, n: (b, n, t)),
        out_shape=jax.ShapeDtypeStruct((batch, N, T), out_dtype),
        compiler_params=_params(3), name=name)(w_t, x)


def _kvnorm_kernel(p_ref, g_ref, c_ref, ct_ref, k_ref, *, d_lat, d_idx):
    xf = p_ref[:, :d_lat]
    c = xf * lax.rsqrt(jnp.mean(jnp.square(xf), axis=-1, keepdims=True) + LN_EPS) * g_ref[...]
    c_ref[0] = c.astype(BF16)
    ct_ref[0, 0] = c.T.astype(BF16)
    k_ref[0] = p_ref[:, d_lat:d_lat + d_idx].astype(BF16)


def _kvnorm(proj_small, g, batch, tk, d_lat, d_idx):
    M, W = proj_small.shape
    T = M // batch
    nt = T // tk
    return pl.pallas_call(
        functools.partial(_kvnorm_kernel, d_lat=d_lat, d_idx=d_idx), grid=(batch, nt),
        in_specs=[pl.BlockSpec((tk, W), lambda b, t: (b * nt + t, 0)),
                  pl.BlockSpec((1, d_lat), lambda b, t: (0, 0))],
        out_specs=[pl.BlockSpec((1, tk, d_lat), lambda b, t: (b, t, 0)),
                   pl.BlockSpec((1, 1, d_lat, tk), lambda b, t: (b, t, 0, 0)),
                   pl.BlockSpec((1, tk, d_idx), lambda b, t: (b, t, 0))],
        out_shape=[jax.ShapeDtypeStruct((batch, T, d_lat), BF16),
                   jax.ShapeDtypeStruct((batch, nt, d_lat, tk), BF16),
                   jax.ShapeDtypeStruct((batch, T, d_idx), BF16)],
        compiler_params=_params(2), name="kv_norm")(proj_small, g)


PACK = 32
PACK_ROWS = 8 * PACK
_XPOSE_MASKS = {16: 0x0000FFFF, 8: 0x00FF00FF, 4: 0x0F0F0F0F, 2: 0x33333333, 1: 0x55555555}


def _bit_transpose(words):
    a = list(words)
    for j in (16, 8, 4, 2, 1):
        m = jnp.int32(_XPOSE_MASKS[j])
        for k in range(PACK):
            if k & j == 0:
                lo, hi = a[k], a[k + j]
                t = (lax.shift_right_logical(lo, jnp.int32(j)) ^ hi) & m
                a[k] = lo ^ (t << j)
                a[k + j] = hi ^ t
    return a


def _rows_below(limit, base):
    nk = jnp.clip((limit - base + 7) >> 3, 0, PACK)
    return jnp.where(nk >= PACK, -1, (jnp.int32(1) << jnp.minimum(nk, PACK - 1)) - 1)


def _topk_kernel(qi_ref, w_ref, k_ref, sel_ref, keys_ref, planes_ref, alive_ref, greater_ref,
                 *, tq, tk, topk):
    b = pl.program_id(0)
    i = pl.program_id(1)
    T = k_ref.shape[1]
    hi = w_ref.shape[1]
    di = k_ref.shape[2]
    R = T // PACK
    n_chunks = (i + 1) * tq // PACK_ROWS
    q_pos = i * tq + lax.broadcasted_iota(jnp.int32, (1, tq), 1)

    @pl.when((b == 0) & (i == 0))
    def _():
        planes_ref[...] = jnp.zeros(planes_ref.shape, jnp.int32)

    def chunk(c, carry):
        for half in range(PACK_ROWS // tk):
            r0 = pl.multiple_of(c * PACK_ROWS + half * tk, tk)
            kc = k_ref[0, pl.ds(r0, tk), :]
            acc = jnp.zeros((tk, tq), F32)
            for h in range(hi):
                s = jnp.dot(kc, qi_ref[0, h * di:(h + 1) * di, :], preferred_element_type=F32)
                acc = acc + w_ref[0, h:h + 1, :] * jnp.maximum(s, 0.0)
            bits = pltpu.bitcast(acc, jnp.int32)
            ukey = bits ^ ((bits >> 31) | INT_MIN)
            k_pos = r0 + lax.broadcasted_iota(jnp.int32, (tk, tq), 0)
            keys_ref[half * tk:(half + 1) * tk, :] = jnp.where(k_pos <= q_pos, ukey, 0)
        p0 = pl.multiple_of(c * 8, 8)
        for lt in range(tq // LANES):
            ls = slice(lt * LANES, (lt + 1) * LANES)
            planes = _bit_transpose([keys_ref[8 * k:8 * k + 8, ls] for k in range(PACK)])
            for bit in range(PACK):
                planes_ref[bit, pl.ds(p0, 8), ls] = planes[bit]
        return carry

    lax.fori_loop(0, n_chunks, chunk, 0)

    prow = lax.broadcasted_iota(jnp.int32, (R, tq), 0)
    base = (prow >> 3) * PACK_ROWS + (prow & 7)
    alive_ref[...] = jnp.where((prow >> 3) < n_chunks, -1, 0)
    greater_ref[...] = jnp.zeros((R, tq), jnp.int32)

    def popcount_rows(words):
        return jnp.sum(lax.population_count(words), axis=0, keepdims=True)

    def select_bit(n, n_greater):
        plane = planes_ref[PACK - 1 - n]
        alive = alive_ref[...]
        ones = alive & plane
        cnt = popcount_rows(ones)
        take = (n_greater + cnt) >= topk
        alive_ref[...] = jnp.where(take, ones, alive & ~plane)
        greater_ref[...] = jnp.where(take, greater_ref[...], greater_ref[...] | ones)
        return jnp.where(take, n_greater, n_greater + cnt)

    n_greater = lax.fori_loop(0, PACK, select_bit, jnp.zeros((1, tq), jnp.int32))
    need = topk - n_greater
    n_tie = popcount_rows(alive_ref[...])

    sel_ref[0] = greater_ref[...] | alive_ref[...]

    @pl.when(jnp.max(n_tie - need) > 0)
    def _():
        nbits = (T - 1).bit_length()
        ties = alive_ref[...]

        def jbody(n, jc):
            cand = jc | jnp.left_shift(jnp.int32(1), nbits - 1 - n)
            f = popcount_rows(ties & _rows_below(cand, base))
            return jnp.where(f < need, cand, jc)

        jc = lax.fori_loop(0, nbits, jbody, jnp.zeros((1, tq), jnp.int32))
        sel_ref[0] = greater_ref[...] | (ties & _rows_below(jc + 1, base))

    sel_ref[0] = sel_ref[0] & _rows_below(q_pos + 1, base)


def _topk_select(qt, w_t, k_idx, tq, qi_row_block, topk):
    batch, hi, T = w_t.shape
    di = k_idx.shape[2]
    tk = 128
    assert T % tq == 0 and tq % PACK_ROWS == 0 and PACK_ROWS % tk == 0 and tq % LANES == 0
    R = T // PACK
    return pl.pallas_call(
        functools.partial(_topk_kernel, tq=tq, tk=tk, topk=topk),
        grid=(batch, T // tq),
        in_specs=[pl.BlockSpec((1, hi * di, tq), lambda b, i: (b, qi_row_block, i)),
                  pl.BlockSpec((1, hi, tq), lambda b, i: (b, 0, i)),
                  pl.BlockSpec((1, T, di), lambda b, i: (b, 0, 0))],
        out_specs=pl.BlockSpec((1, R, tq), lambda b, i: (b, 0, i)),
        out_shape=jax.ShapeDtypeStruct((batch, R, T), jnp.int32),
        scratch_shapes=[pltpu.VMEM((PACK_ROWS, tq), jnp.int32), pltpu.VMEM((PACK, R, tq), jnp.int32),
                        pltpu.VMEM((R, tq), jnp.int32), pltpu.VMEM((R, tq), jnp.int32)],
        compiler_params=_params(2), name="indexer_topk")(qt, w_t, k_idx)


def _bias_kernel(rel_ref, o_ref, *, tq, n_buckets):
    h = pl.program_id(0)
    max_exact = n_buckets // 2
    kk = lax.broadcasted_iota(jnp.int32, (tq, tq), 0)
    qq = lax.broadcasted_iota(jnp.int32, (tq, tq), 1)
    for r in range(2):
        n = jnp.maximum(r * tq + qq - kk, 0)
        nf = jnp.maximum(n, 1).astype(F32)
        large = max_exact + (jnp.log(nf / max_exact) / math.log(MAX_DIST / max_exact)
                             * (n_buckets - max_exact)).astype(jnp.int32)
        bucket = jnp.where(n < max_exact, n, jnp.minimum(large, n_buckets - 1))
        tile = jnp.zeros((tq, tq), F32)
        for k in range(n_buckets):
            tile = jnp.where(bucket == k, rel_ref[k, h], tile)
        o_ref[0, r] = tile * LOG2E


def _bias_tiles(rel_bias, tq):
    n_buckets, n_heads = rel_bias.shape
    return pl.pallas_call(
        functools.partial(_bias_kernel, tq=tq, n_buckets=n_buckets), grid=(n_heads,),
        in_specs=[pl.BlockSpec(memory_space=pltpu.SMEM)],
        out_specs=pl.BlockSpec((1, 2, tq, tq), lambda h: (h, 0, 0, 0)),
        out_shape=jax.ShapeDtypeStruct((n_heads, 2, tq, tq), F32),
        compiler_params=_params(1), name="rel_bias_tiles")(rel_bias)


def _attn_kernel(rel_ref, q_ref, c_ref, ct_ref, sel_ref, bias_ref, wuv_ref, gate_ref, o_ref,
                 m_ref, l_ref, acc_ref, *, tq, n_group):
    i = pl.program_id(1)
    g = pl.program_id(2)
    d_lat = c_ref.shape[2]
    d_head = wuv_ref.shape[2]
    m_ref[...] = jnp.full(m_ref.shape, NEG, F32)
    l_ref[...] = jnp.zeros(l_ref.shape, F32)
    acc_ref[...] = jnp.zeros(acc_ref.shape, F32)

    def step(j, bias_slot):
        r0 = pl.multiple_of(j * tq, tq)
        cj = c_ref[0, pl.ds(r0, tq), :]
        ctj = ct_ref[0, j]
        words = sel_ref[0, pl.ds(pl.multiple_of(j * 8, 8), 8), :]
        mk = jnp.concatenate([jnp.where((words << (PACK - 1 - k)) < 0, 0.0, NEG) for k in range(PACK)],
                             axis=0)
        for hh in range(n_group):
            s = jnp.dot(cj, q_ref[0, hh * d_lat:(hh + 1) * d_lat, :], preferred_element_type=F32) + mk
            if bias_slot is not None:
                s = s + bias_ref[hh, bias_slot]
            m_old = m_ref[hh, 0:1, :]
            m_new = jnp.maximum(m_old, jnp.max(s, axis=0, keepdims=True))
            alpha = jnp.exp2(m_old - m_new)
            p = jnp.exp2(s - m_new)
            l_ref[hh, 0:1, :] = alpha * l_ref[hh, 0:1, :] + jnp.sum(p, axis=0, keepdims=True)
            m_ref[hh, 0:1, :] = m_new
            acc_ref[hh] = alpha * acc_ref[hh] + jnp.dot(ctj, p.astype(BF16), preferred_element_type=F32)

    def far(j, carry):
        step(j, None)
        return carry

    lax.fori_loop(0, jnp.maximum(i - 1, 0), far, 0)
    for hh in range(n_group):
        m_ref[hh, 0:1, :] = m_ref[hh, 0:1, :] + rel_ref[rel_ref.shape[0] - 1, g * n_group + hh] * LOG2E

    def near(n, carry):
        step(i - 1 + n, 1 - n)
        return carry

    lax.fori_loop(jnp.where(i == 0, 1, 0), 2, near, 0)

    for hh in range(n_group):
        o = acc_ref[hh] / l_ref[hh, 0:1, :]
        val = jnp.dot(o.T.astype(BF16), wuv_ref[hh], preferred_element_type=F32)
        gt = gate_ref[:, hh * d_head:(hh + 1) * d_head]
        o_ref[:, hh * d_head:(hh + 1) * d_head] = (val * (gt * _sigmoid(gt))).astype(o_ref.dtype)


def _attention(rel_bias, qt, c, ct, sel, bias, w_uv, proj, gate_col_block, tq, n_group):
    batch, T, d_lat = c.shape
    n_heads, _, d_head = w_uv.shape
    nq = T // tq
    ng = n_heads // n_group
    assert tq >= MAX_DIST and tq == PACK_ROWS and d_head % LANES == 0 and n_heads % n_group == 0
    once = pl.Buffered(1)
    return pl.pallas_call(
        functools.partial(_attn_kernel, tq=tq, n_group=n_group), grid=(batch, nq, ng),
        in_specs=[pl.BlockSpec(memory_space=pltpu.SMEM),
                  pl.BlockSpec((1, n_group * d_lat, tq), lambda b, i, g: (b, g, i)),
                  pl.BlockSpec((1, T, d_lat), lambda b, i, g: (b, 0, 0), pipeline_mode=once),
                  pl.BlockSpec((1, nq, d_lat, tq), lambda b, i, g: (b, 0, 0, 0), pipeline_mode=once),
                  pl.BlockSpec((1, T // PACK, tq), lambda b, i, g: (b, 0, i)),
                  pl.BlockSpec((n_group, 2, tq, tq), lambda b, i, g: (g, 0, 0, 0), pipeline_mode=once),
                  pl.BlockSpec((n_group, d_lat, d_head), lambda b, i, g: (g, 0, 0), pipeline_mode=once),
                  pl.BlockSpec((tq, n_group * d_head),
                               lambda b, i, g: (b * nq + i, gate_col_block * ng + g))],
        out_specs=pl.BlockSpec((tq, n_group * d_head), lambda b, i, g: (b * nq + i, g)),
        out_shape=jax.ShapeDtypeStruct((batch * T, n_heads * d_head), BF16),
        scratch_shapes=[pltpu.VMEM((n_group, 8, tq), F32), pltpu.VMEM((n_group, 8, tq), F32),
                        pltpu.VMEM((n_group, d_lat, tq), F32)],
        compiler_params=_params(3), name="sparse_attention")(
            rel_bias, qt, c, ct, sel, bias, w_uv, proj)


def _rglru_kernel(x_ref, gate_ref, cw_ref, cb_ref, wg_ref, ba_ref, bx_ref, lam_ref, o_ref,
                  xbuf_ref, h_ref, *, tt, conv_w):
    t = pl.program_id(2)
    dc = x_ref.shape[1]
    blk = wg_ref.shape[1]
    pad = 8

    @pl.when(t == 0)
    def _():
        xbuf_ref[0:pad, :] = jnp.zeros((pad, dc), F32)
        h_ref[...] = jnp.zeros(h_ref.shape, F32)

    xbuf_ref[pad:pad + tt, :] = x_ref[...]
    row = lax.broadcasted_iota(jnp.int32, (tt, blk), 0)
    first = (row + t * tt) == 0
    sub = lax.broadcasted_iota(jnp.int32, (SUBLANES, blk), 0)

    for n in range(dc // blk):
        cs = slice(n * blk, (n + 1) * blk)
        xc = cb_ref[:, cs]
        for k in range(conv_w):
            off = pad - (conv_w - 1) + k
            xc = xc + cw_ref[k:k + 1, cs] * xbuf_ref[off:off + tt, cs]
        g = jnp.dot(xc.astype(BF16), wg_ref[n], preferred_element_type=F32)
        r_gate = _sigmoid(g[:, :blk] + ba_ref[:, cs])
        i_gate = _sigmoid(g[:, blk:] + bx_ref[:, cs])
        lam = lam_ref[:, cs]
        softplus_neg = jnp.maximum(-lam, 0.0) + jnp.log(1.0 + jnp.exp(-jnp.abs(lam)))
        log_a = -LRU_C * r_gate * softplus_neg
        a = jnp.exp(log_a)
        mult = jnp.where(first, 1.0, jnp.sqrt(1.0 - a * a))
        bv = mult * (i_gate * xc)
        carry = jnp.broadcast_to(h_ref[0:1, cs], (SUBLANES, blk))
        tiles = []
        for v in range(tt // SUBLANES):
            av = a[v * SUBLANES:(v + 1) * SUBLANES]
            bw = bv[v * SUBLANES:(v + 1) * SUBLANES]
            s = 1
            while s < SUBLANES:
                valid = sub >= s
                bw = jnp.where(valid, av * pltpu.roll(bw, s, 0) + bw, bw)
                av = jnp.where(valid, av * pltpu.roll(av, s, 0), av)
                s *= 2
            hv = bw + av * carry
            carry = jnp.broadcast_to(hv[SUBLANES - 1:SUBLANES], (SUBLANES, blk))
            tiles.append(hv)
        hh = jnp.concatenate(tiles, axis=0)
        h_ref[0:1, cs] = hh[tt - 1:tt, :]
        gt = gate_ref[:, cs]
        o_ref[:, cs] = (hh * (gt * _sigmoid(gt))).astype(o_ref.dtype)

    xbuf_ref[0:pad, :] = xbuf_ref[tt:tt + pad, :]


def _rglru(proj, x_col_block, gate_col_block, conv_w, conv_b, w_gates, b_a, b_x, lam, batch, tt, dc):
    M = proj.shape[0]
    T = M // batch
    d_rnn = conv_w.shape[1]
    cw = conv_w.shape[0]
    blk = w_gates.shape[1]
    tt, dc = min(tt, T), min(dc, d_rnn)
    assert T % tt == 0 and d_rnn % dc == 0 and dc % blk == 0 and cw - 1 <= 8 <= tt
    nt, nd = T // tt, d_rnn // dc
    vec = lambda: pl.BlockSpec((1, dc), lambda b, d, t: (0, d))
    return pl.pallas_call(
        functools.partial(_rglru_kernel, tt=tt, conv_w=cw), grid=(batch, nd, nt),
        in_specs=[pl.BlockSpec((tt, dc), lambda b, d, t: (b * nt + t, x_col_block * nd + d)),
                  pl.BlockSpec((tt, dc), lambda b, d, t: (b * nt + t, gate_col_block * nd + d)),
                  pl.BlockSpec((cw, dc), lambda b, d, t: (0, d)),
                  vec(),
                  pl.BlockSpec((dc // blk, blk, 2 * blk), lambda b, d, t: (d, 0, 0)),
                  vec(), vec(), vec()],
        out_specs=pl.BlockSpec((tt, dc), lambda b, d, t: (b * nt + t, d)),
        out_shape=jax.ShapeDtypeStruct((M, d_rnn), BF16),
        scratch_shapes=[pltpu.VMEM((tt + 8, dc), F32), pltpu.VMEM((8, dc), F32)],
        compiler_params=_params(3), name="rglru")(
            proj, proj, conv_w, conv_b, w_gates, b_a, b_x, lam)


def _merge_kernel(a_ref, hb_ref, wa_ref, wb_ref, ga_ref, gb_ref, o_ref):
    br_a = jnp.dot(a_ref[...], wa_ref[...], preferred_element_type=F32)
    br_b = jnp.dot(hb_ref[...], wb_ref[...], preferred_element_type=F32)
    o_ref[...] = (_sigmoid(ga_ref[...]) * br_a
                  + _sigmoid(gb_ref[...]) * br_b).astype(o_ref.dtype)


def _merge(a, hb, wa, wb, proj, ga_col_block, gb_col_block, tm, tn):
    M = a.shape[0]
    d_model = wa.shape[1]
    tm, tn = min(tm, M), min(tn, d_model)
    assert M % tm == 0 and d_model % tn == 0
    nn = d_model // tn
    return pl.pallas_call(
        _merge_kernel, grid=(M // tm, nn),
        in_specs=[pl.BlockSpec((tm, a.shape[1]), lambda i, j: (i, 0)),
                  pl.BlockSpec((tm, hb.shape[1]), lambda i, j: (i, 0)),
                  pl.BlockSpec((wa.shape[0], tn), lambda i, j: (0, j)),
                  pl.BlockSpec((wb.shape[0], tn), lambda i, j: (0, j)),
                  pl.BlockSpec((tm, tn), lambda i, j: (i, ga_col_block * nn + j)),
                  pl.BlockSpec((tm, tn), lambda i, j: (i, gb_col_block * nn + j))],
        out_specs=pl.BlockSpec((tm, tn), lambda i, j: (i, j)),
        out_shape=jax.ShapeDtypeStruct((M, d_model), BF16),
        compiler_params=_params(2), name="branch_merge")(a, hb, wa, wb, proj, proj)


def _out_kernel(m_ref, w_ref, x_ref, g_ref, b_ref, o_ref, *, alpha):
    sub = jnp.dot(m_ref[...], w_ref[...], preferred_element_type=F32)
    y = alpha * x_ref[...] + sub
    mu = jnp.mean(y, axis=-1, keepdims=True)
    var = jnp.mean(jnp.square(y - mu), axis=-1, keepdims=True)
    o_ref[...] = (y - mu) * lax.rsqrt(var + LN_EPS) * g_ref[...] + b_ref[...]


def _out_norm(merged, w_out, x, g, b, alpha, tm):
    M, d = x.shape
    tm = min(tm, M)
    assert M % tm == 0
    return pl.pallas_call(
        functools.partial(_out_kernel, alpha=alpha), grid=(M // tm,),
        in_specs=[pl.BlockSpec((tm, d), lambda i: (i, 0)),
                  pl.BlockSpec((d, d), lambda i: (0, 0)),
                  pl.BlockSpec((tm, d), lambda i: (i, 0)),
                  pl.BlockSpec((1, d), lambda i: (0, 0)),
                  pl.BlockSpec((1, d), lambda i: (0, 0))],
        out_specs=pl.BlockSpec((tm, d), lambda i: (i, 0)),
        out_shape=jax.ShapeDtypeStruct((M, d), F32),
        compiler_params=_params(1), name="out_proj_layernorm")(merged, w_out, x, g, b)


def _layer(x2, batch, w_in, kv_norm_g, w_uv, w_branch_a, conv_w, conv_b, w_gate_a, b_gate_a,
           w_gate_x, b_gate_x, lru_lambda, w_branch_b, rel_bias, w_out, ln_g, ln_b, alpha):
    M, d_model = x2.shape
    T = M // batch
    n_heads, d_lat, d_head = w_uv.shape
    d_attn = n_heads * d_head
    d_rnn = conv_w.shape[1]
    hi, di = N_HEADS_IDX, D_IDX
    sizes = (n_heads * d_lat, d_lat, d_attn, hi * di, di, hi, d_rnn, d_rnn, d_model, d_model)
    assert sum(sizes) == w_in.shape[1]
    offs = [0]
    for s in sizes:
        offs.append(offs[-1] + s)
    seg = lambda k: w_in[:, offs[k]:offs[k + 1]]
    assert d_attn == d_rnn == d_model and d_model % LANES == 0
    tq = TOPK_MAX
    topk = min(TOPK_MAX, T // 4)
    assert T % tq == 0 and (n_heads * d_lat) % (hi * di) == 0

    xb = x2.astype(BF16)
    w_rows = jnp.concatenate([seg(2), seg(6), seg(7), seg(8), seg(9)], axis=1).astype(BF16)
    proj = _matmul(xb, w_rows, 1024, 1024, F32)
    small_w = d_lat + di
    small_pad = -small_w % LANES
    w_small = jnp.pad(jnp.concatenate([seg(1), seg(4)], axis=1), ((0, 0), (0, small_pad))).astype(BF16)
    proj_small = _matmul(xb, w_small, 1024, small_w + small_pad, F32)
    att_scale = d_lat ** -0.5 * LOG2E
    idx_scale = (di ** -0.5) * (hi ** -0.5)
    w_qt = jnp.concatenate([seg(0) * att_scale, seg(3)], axis=1).T.astype(BF16)
    qt = _matmul_nt(w_qt, xb, batch, 512, 1024, BF16, "proj_queries_t")
    w_t = _matmul_nt((seg(5) * idx_scale).T.astype(BF16), xb, batch, hi, 1024, F32, "proj_idx_w_t")

    c, ct, k_idx = _kvnorm(proj_small, kv_norm_g.reshape(1, d_lat), batch, tq, d_lat, di)
    sel = _topk_select(qt, w_t, k_idx, tq, (n_heads * d_lat) // (hi * di), topk)
    bias = _bias_tiles(rel_bias, tq)
    a = _attention(rel_bias, qt, c, ct, sel, bias, w_uv.astype(BF16), proj, 0, tq, n_heads)

    w_gates = jnp.concatenate([w_gate_a, w_gate_x], axis=2).astype(BF16)
    row = lambda v: v.reshape(1, -1)
    hb = _rglru(proj, 1, 2, conv_w, row(conv_b), w_gates, row(b_gate_a), row(b_gate_x),
                row(lru_lambda), batch, 256, 512)

    merged = _merge(a, hb, w_branch_a.astype(BF16), w_branch_b.astype(BF16), proj, 3, 4, 1024, 512)
    return _out_norm(merged, w_out.astype(BF16), x2, row(ln_g), row(ln_b), alpha, 512)


def kernel(x, w_in, kv_norm_g, w_uv, w_branch_a, conv_w, conv_b, w_gate_a, b_gate_a, w_gate_x,
           b_gate_x, lru_lambda, w_branch_b, rel_bias, w_out, ln_g, ln_b):
    batch, T, d_model = x.shape
    depth = w_in.shape[0]
    alpha = (2 * depth) ** 0.25
    x2 = x.reshape(batch * T, d_model)
    for l in range(depth):
        x2 = _layer(x2, batch, w_in[l], kv_norm_g[l], w_uv[l], w_branch_a[l], conv_w[l], conv_b[l],
                    w_gate_a[l], b_gate_a[l], w_gate_x[l], b_gate_x[l], lru_lambda[l], w_branch_b[l],
                    rel_bias, w_out[l], ln_g[l], ln_b[l], alpha)
    return x2.reshape(batch, T, d_model)
```

```python
import functools
import math

import jax
import jax.numpy as jnp
from jax import lax
from jax.experimental import pallas as pl
from jax.experimental.pallas import tpu as pltpu

N_HEADS_IDX = 16
D_IDX = 64
TOPK_MAX = 256
MAX_DIST = 128
LRU_C = 8.0
LN_EPS = 1e-5

LANES = 128
SUBLANES = 8
V7X_VMEM_LIMIT_BYTES = 56 * 1024 * 1024

NEG = -1e30
INT_MIN = -(2 ** 31)
LOG2E = math.log2(math.e)

F32 = jnp.float32
BF16 = jnp.bfloat16


def _sigmoid(x):
    return 0.5 * jnp.tanh(0.5 * x) + 0.5


def _params(n_grid):
    return pltpu.CompilerParams(dimension_semantics=("arbitrary",) * n_grid,
                                vmem_limit_bytes=V7X_VMEM_LIMIT_BYTES)


def _mm_kernel(x_ref, w_ref, o_ref):
    o_ref[...] = jnp.dot(x_ref[...], w_ref[...], preferred_element_type=F32).astype(o_ref.dtype)


def _matmul(x, w, tm, tn, out_dtype):
    M, K = x.shape
    N = w.shape[1]
    tm, tn = min(tm, M), min(tn, N)
    assert M % tm == 0 and N % tn == 0
    return pl.pallas_call(
        _mm_kernel, grid=(M // tm, N // tn),
        in_specs=[pl.BlockSpec((tm, K), lambda i, j: (i, 0)),
                  pl.BlockSpec((K, tn), lambda i, j: (0, j))],
        out_specs=pl.BlockSpec((tm, tn), lambda i, j: (i, j)),
        out_shape=jax.ShapeDtypeStruct((M, N), out_dtype),
        compiler_params=_params(2), name="proj_rows")(x, w)


def _mm_nt_kernel(w_ref, x_ref, o_ref):
    o_ref[0] = lax.dot_general(w_ref[...], x_ref[...], (((1,), (1,)), ((), ())),
                               preferred_element_type=F32).astype(o_ref.dtype)


def _matmul_nt(w_t, x, batch, tn, tt, out_dtype, name):
    N, K = w_t.shape
    M = x.shape[0]
    T = M // batch
    tn, tt = min(tn, N), min(tt, T)
    assert N % tn == 0 and T % tt == 0
    nt = T // tt
    return pl.pallas_call(
        _mm_nt_kernel, grid=(batch, nt, N // tn),
        in_specs=[pl.BlockSpec((tn, K), lambda b, t, n: (n, 0)),
                  pl.BlockSpec((tt, K), lambda b, t, n: (b * nt + t, 0))],
        out_specs=pl.BlockSpec((1, tn, tt), lambda b, t, n: (b, n, t)),
        out_shape=jax.ShapeDtypeStruct((batch, N, T), out_dtype),
        compiler_params=_params(3), name=name)(w_t, x)


def _kvnorm_kernel(p_ref, g_ref, c_ref, ct_ref, k_ref, *, d_lat, d_idx):
    xf = p_ref[:, :d_lat]
    c = xf * lax.rsqrt(jnp.mean(jnp.square(xf), axis=-1, keepdims=True) + LN_EPS) * g_ref[...]
    c_ref[0] = c.astype(BF16)
    ct_ref[0, 0] = c.T.astype(BF16)
    k_ref[0] = p_ref[:, d_lat:d_lat + d_idx].astype(BF16)


def _kvnorm(proj_small, g, batch, tk, d_lat, d_idx):
    M, W = proj_small.shape
    T = M // batch
    nt = T // tk
    return pl.pallas_call(
        functools.partial(_kvnorm_kernel, d_lat=d_lat, d_idx=d_idx), grid=(batch, nt),
        in_specs=[pl.BlockSpec((tk, W), lambda b, t: (b * nt + t, 0)),
                  pl.BlockSpec((1, d_lat), lambda b, t: (0, 0))],
        out_specs=[pl.BlockSpec((1, tk, d_lat), lambda b, t: (b, t, 0)),
                   pl.BlockSpec((1, 1, d_lat, tk), lambda b, t: (b, t, 0, 0)),
                   pl.BlockSpec((1, tk, d_idx), lambda b, t: (b, t, 0))],
        out_shape=[jax.ShapeDtypeStruct((batch, T, d_lat), BF16),
                   jax.ShapeDtypeStruct((batch, nt, d_lat, tk), BF16),
                   jax.ShapeDtypeStruct((batch, T, d_idx), BF16)],
        compiler_params=_params(2), name="kv_norm")(proj_small, g)


PACK = 32
PACK_ROWS = 8 * PACK
_XPOSE_MASKS = {16: 0x0000FFFF, 8: 0x00FF00FF, 4: 0x0F0F0F0F, 2: 0x33333333, 1: 0x55555555}


def _bit_transpose(words):
    a = list(words)
    for j in (16, 8, 4, 2, 1):
        m = jnp.int32(_XPOSE_MASKS[j])
        for k in range(PACK):
            if k & j == 0:
                lo, hi = a[k], a[k + j]
                t = (lax.shift_right_logical(lo, jnp.int32(j)) ^ hi) & m
                a[k] = lo ^ (t << j)
                a[k + j] = hi ^ t
    return a


def _rows_below(limit, base):
    nk = jnp.clip((limit - base + 7) >> 3, 0, PACK)
    return jnp.where(nk >= PACK, -1, (jnp.int32(1) << jnp.minimum(nk, PACK - 1)) - 1)


def _topk_kernel(qi_ref, w_ref, k_ref, sel_ref, keys_ref, planes_ref, alive_ref, greater_ref,
                 *, tq, tk, topk):
    b = pl.program_id(0)
    i = pl.program_id(1)
    T = k_ref.shape[1]
    hi = w_ref.shape[1]
    di = k_ref.shape[2]
    R = T // PACK
    n_chunks = (i + 1) * tq // PACK_ROWS
    q_pos = i * tq + lax.broadcasted_iota(jnp.int32, (1, tq), 1)

    @pl.when((b == 0) & (i == 0))
    def _():
        planes_ref[...] = jnp.zeros(planes_ref.shape, jnp.int32)

    def chunk(c, slot):
        k0 = slot * PACK_ROWS
        for half in range(PACK_ROWS // tk):
            r0 = pl.multiple_of(c * PACK_ROWS + half * tk, tk)
            kc = k_ref[0, pl.ds(r0, tk), :]
            acc = jnp.zeros((tk, tq), F32)
            for h in range(hi):
                s = jnp.dot(kc, qi_ref[0, h * di:(h + 1) * di, :], preferred_element_type=F32)
                acc = acc + w_ref[0, h:h + 1, :] * jnp.maximum(s, 0.0)
            bits = pltpu.bitcast(acc, jnp.int32)
            ukey = bits ^ ((bits >> 31) | INT_MIN)
            k_pos = r0 + lax.broadcasted_iota(jnp.int32, (tk, tq), 0)
            keys_ref[k0 + half * tk:k0 + (half + 1) * tk, :] = jnp.where(k_pos <= q_pos, ukey, 0)
        p0 = pl.multiple_of(c * 8, 8)
        for lt in range(tq // LANES):
            ls = slice(lt * LANES, (lt + 1) * LANES)
            planes = _bit_transpose([keys_ref[k0 + 8 * k:k0 + 8 * k + 8, ls] for k in range(PACK)])
            for bit in range(PACK):
                planes_ref[bit, pl.ds(p0, 8), ls] = planes[bit]

    def chunk_pair(n, carry):
        chunk(2 * n, 0)
        chunk(2 * n + 1, 1)
        return carry

    def chunk_single(c, carry):
        chunk(c, 0)
        return carry

    lax.fori_loop(0, n_chunks // 2, chunk_pair, 0)
    lax.fori_loop(2 * (n_chunks // 2), n_chunks, chunk_single, 0)

    prow = lax.broadcasted_iota(jnp.int32, (R, tq), 0)
    base = (prow >> 3) * PACK_ROWS + (prow & 7)
    alive_ref[...] = jnp.where((prow >> 3) < n_chunks, -1, 0)
    greater_ref[...] = jnp.zeros((R, tq), jnp.int32)

    def popcount_rows(words):
        return jnp.sum(lax.population_count(words), axis=0, keepdims=True)

    def select_bit(n, n_greater):
        plane = planes_ref[PACK - 1 - n]
        alive = alive_ref[...]
        ones = alive & plane
        cnt = popcount_rows(ones)
        take = (n_greater + cnt) >= topk
        alive_ref[...] = jnp.where(take, ones, alive & ~plane)
        greater_ref[...] = jnp.where(take, greater_ref[...], greater_ref[...] | ones)
        return jnp.where(take, n_greater, n_greater + cnt)

    n_greater = lax.fori_loop(0, PACK, select_bit, jnp.zeros((1, tq), jnp.int32))
    need = topk - n_greater
    n_tie = popcount_rows(alive_ref[...])

    sel_ref[0] = greater_ref[...] | alive_ref[...]

    @pl.when(jnp.max(n_tie - need) > 0)
    def _():
        nbits = (T - 1).bit_length()
        ties = alive_ref[...]

        def jbody(n, jc):
            cand = jc | jnp.left_shift(jnp.int32(1), nbits - 1 - n)
            f = popcount_rows(ties & _rows_below(cand, base))
            return jnp.where(f < need, cand, jc)

        jc = lax.fori_loop(0, nbits, jbody, jnp.zeros((1, tq), jnp.int32))
        sel_ref[0] = greater_ref[...] | (ties & _rows_below(jc + 1, base))

    sel_ref[0] = sel_ref[0] & _rows_below(q_pos + 1, base)


def _topk_select(qt, w_t, k_idx, tq, qi_row_block, topk):
    batch, hi, T = w_t.shape
    di = k_idx.shape[2]
    tk = 128
    assert T % tq == 0 and tq % PACK_ROWS == 0 and PACK_ROWS % tk == 0 and tq % LANES == 0
    R = T // PACK
    return pl.pallas_call(
        functools.partial(_topk_kernel, tq=tq, tk=tk, topk=topk),
        grid=(batch, T // tq),
        in_specs=[pl.BlockSpec((1, hi * di, tq), lambda b, i: (b, qi_row_block, i)),
                  pl.BlockSpec((1, hi, tq), lambda b, i: (b, 0, i)),
                  pl.BlockSpec((1, T, di), lambda b, i: (b, 0, 0))],
        out_specs=pl.BlockSpec((1, R, tq), lambda b, i: (b, 0, i)),
        out_shape=jax.ShapeDtypeStruct((batch, R, T), jnp.int32),
        scratch_shapes=[pltpu.VMEM((2 * PACK_ROWS, tq), jnp.int32), pltpu.VMEM((PACK, R, tq), jnp.int32),
                        pltpu.VMEM((R, tq), jnp.int32), pltpu.VMEM((R, tq), jnp.int32)],
        compiler_params=_params(2), name="indexer_topk")(qt, w_t, k_idx)


def _bias_kernel(rel_ref, o_ref, *, tq, n_buckets):
    h = pl.program_id(0)
    max_exact = n_buckets // 2
    kk = lax.broadcasted_iota(jnp.int32, (tq, tq), 0)
    qq = lax.broadcasted_iota(jnp.int32, (tq, tq), 1)
    for r in range(2):
        n = jnp.maximum(r * tq + qq - kk, 0)
        nf = jnp.maximum(n, 1).astype(F32)
        large = max_exact + (jnp.log(nf / max_exact) / math.log(MAX_DIST / max_exact)
                             * (n_buckets - max_exact)).astype(jnp.int32)
        bucket = jnp.where(n < max_exact, n, jnp.minimum(large, n_buckets - 1))
        tile = jnp.zeros((tq, tq), F32)
        for k in range(n_buckets):
            tile = jnp.where(bucket == k, rel_ref[k, h], tile)
        o_ref[0, r] = tile * LOG2E


def _bias_tiles(rel_bias, tq):
    n_buckets, n_heads = rel_bias.shape
    return pl.pallas_call(
        functools.partial(_bias_kernel, tq=tq, n_buckets=n_buckets), grid=(n_heads,),
        in_specs=[pl.BlockSpec(memory_space=pltpu.SMEM)],
        out_specs=pl.BlockSpec((1, 2, tq, tq), lambda h: (h, 0, 0, 0)),
        out_shape=jax.ShapeDtypeStruct((n_heads, 2, tq, tq), F32),
        compiler_params=_params(1), name="rel_bias_tiles")(rel_bias)


def _attn_kernel(rel_ref, q_ref, c_ref, ct_ref, sel_ref, bias_ref, wuv_ref, gate_ref, o_ref,
                 m_ref, l_ref, acc_ref, *, tq, n_group):
    i = pl.program_id(1)
    g = pl.program_id(2)
    d_lat = c_ref.shape[2]
    d_head = wuv_ref.shape[2]
    m_ref[...] = jnp.full(m_ref.shape, NEG, F32)
    l_ref[...] = jnp.zeros(l_ref.shape, F32)
    acc_ref[...] = jnp.zeros(acc_ref.shape, F32)

    def step(j, bias_slot, n_sub=1):
        r0 = pl.multiple_of(j * tq, tq)
        cj = c_ref[0, pl.ds(r0, n_sub * tq), :]
        ctj = jnp.concatenate([ct_ref[0, j + u] for u in range(n_sub)], axis=1)
        words = sel_ref[0, pl.ds(pl.multiple_of(j * 8, 8), n_sub * 8), :]
        mk = jnp.concatenate([jnp.where((words[8 * u:8 * u + 8] << (PACK - 1 - k)) < 0, 0.0, NEG)
                              for u in range(n_sub) for k in range(PACK)], axis=0)
        for hh in range(n_group):
            s = jnp.dot(cj, q_ref[0, hh * d_lat:(hh + 1) * d_lat, :], preferred_element_type=F32) + mk
            if bias_slot is not None:
                s = s + bias_ref[hh, bias_slot]
            m_old = m_ref[hh, 0:1, :]
            m_new = jnp.maximum(m_old, jnp.max(s, axis=0, keepdims=True))
            alpha = jnp.exp2(m_old - m_new)
            p = jnp.exp2(s - m_new)
            l_ref[hh, 0:1, :] = alpha * l_ref[hh, 0:1, :] + jnp.sum(p, axis=0, keepdims=True)
            m_ref[hh, 0:1, :] = m_new
            acc_ref[hh] = alpha * acc_ref[hh] + jnp.dot(ctj, p.astype(BF16), preferred_element_type=F32)

    n_far = jnp.maximum(i - 1, 0)

    def far_pair(k, carry):
        step(2 * k, None, n_sub=2)
        return carry

    def far_single(j, carry):
        step(j, None)
        return carry

    lax.fori_loop(0, n_far // 2, far_pair, 0)
    lax.fori_loop(2 * (n_far // 2), n_far, far_single, 0)
    for hh in range(n_group):
        m_ref[hh, 0:1, :] = m_ref[hh, 0:1, :] + rel_ref[rel_ref.shape[0] - 1, g * n_group + hh] * LOG2E

    def near(n, carry):
        step(i - 1 + n, 1 - n)
        return carry

    lax.fori_loop(jnp.where(i == 0, 1, 0), 2, near, 0)

    for hh in range(n_group):
        o = acc_ref[hh] / l_ref[hh, 0:1, :]
        val = jnp.dot(o.T.astype(BF16), wuv_ref[hh], preferred_element_type=F32)
        gt = gate_ref[:, hh * d_head:(hh + 1) * d_head]
        o_ref[:, hh * d_head:(hh + 1) * d_head] = (val * (gt * _sigmoid(gt))).astype(o_ref.dtype)


def _attention(rel_bias, qt, c, ct, sel, bias, w_uv, proj, gate_col_block, tq, n_group):
    batch, T, d_lat = c.shape
    n_heads, _, d_head = w_uv.shape
    nq = T // tq
    ng = n_heads // n_group
    assert tq >= MAX_DIST and tq == PACK_ROWS and d_head % LANES == 0 and n_heads % n_group == 0
    once = pl.Buffered(1)
    return pl.pallas_call(
        functools.partial(_attn_kernel, tq=tq, n_group=n_group), grid=(batch, nq, ng),
        in_specs=[pl.BlockSpec(memory_space=pltpu.SMEM),
                  pl.BlockSpec((1, n_group * d_lat, tq), lambda b, i, g: (b, g, i)),
                  pl.BlockSpec((1, T, d_lat), lambda b, i, g: (b, 0, 0), pipeline_mode=once),
                  pl.BlockSpec((1, nq, d_lat, tq), lambda b, i, g: (b, 0, 0, 0), pipeline_mode=once),
                  pl.BlockSpec((1, T // PACK, tq), lambda b, i, g: (b, 0, i)),
                  pl.BlockSpec((n_group, 2, tq, tq), lambda b, i, g: (g, 0, 0, 0), pipeline_mode=once),
                  pl.BlockSpec((n_group, d_lat, d_head), lambda b, i, g: (g, 0, 0), pipeline_mode=once),
                  pl.BlockSpec((tq, n_group * d_head),
                               lambda b, i, g: (b * nq + i, gate_col_block * ng + g))],
        out_specs=pl.BlockSpec((tq, n_group * d_head), lambda b, i, g: (b * nq + i, g)),
        out_shape=jax.ShapeDtypeStruct((batch * T, n_heads * d_head), BF16),
        scratch_shapes=[pltpu.VMEM((n_group, SUBLANES, tq), F32), pltpu.VMEM((n_group, SUBLANES, tq), F32),
                        pltpu.VMEM((n_group, d_lat, tq), F32)],
        compiler_params=_params(3), name="sparse_attention")(
            rel_bias, qt, c, ct, sel, bias, w_uv, proj)


def _rglru_kernel(x_ref, gate_ref, cw_ref, cb_ref, wg_ref, ba_ref, bx_ref, lam_ref, o_ref,
                  xbuf_ref, h_ref, *, tt, conv_w):
    t = pl.program_id(2)
    dc = x_ref.shape[1]
    blk = wg_ref.shape[1]
    pad = 8

    @pl.when(t == 0)
    def _():
        xbuf_ref[0:pad, :] = jnp.zeros((pad, dc), F32)
        h_ref[...] = jnp.zeros(h_ref.shape, F32)

    xbuf_ref[pad:pad + tt, :] = x_ref[...]
    row = lax.broadcasted_iota(jnp.int32, (tt, blk), 0)
    first = (row + t * tt) == 0
    sub = lax.broadcasted_iota(jnp.int32, (SUBLANES, blk), 0)

    for n in range(dc // blk):
        cs = slice(n * blk, (n + 1) * blk)
        xc = cb_ref[:, cs]
        for k in range(conv_w):
            off = pad - (conv_w - 1) + k
            xc = xc + cw_ref[k:k + 1, cs] * xbuf_ref[off:off + tt, cs]
        g = jnp.dot(xc.astype(BF16), wg_ref[n], preferred_element_type=F32)
        r_gate = _sigmoid(g[:, :blk] + ba_ref[:, cs])
        i_gate = _sigmoid(g[:, blk:] + bx_ref[:, cs])
        lam = lam_ref[:, cs]
        softplus_neg = jnp.maximum(-lam, 0.0) + jnp.log(1.0 + jnp.exp(-jnp.abs(lam)))
        log_a = -LRU_C * r_gate * softplus_neg
        a = jnp.exp(log_a)
        mult = jnp.where(first, 1.0, jnp.sqrt(1.0 - a * a))
        bv = mult * (i_gate * xc)
        carry = jnp.broadcast_to(h_ref[0:1, cs], (SUBLANES, blk))
        tiles = []
        for v in range(tt // SUBLANES):
            av = a[v * SUBLANES:(v + 1) * SUBLANES]
            bw = bv[v * SUBLANES:(v + 1) * SUBLANES]
            s = 1
            while s < SUBLANES:
                valid = sub >= s
                bw = jnp.where(valid, av * pltpu.roll(bw, s, 0) + bw, bw)
                av = jnp.where(valid, av * pltpu.roll(av, s, 0), av)
                s *= 2
            hv = bw + av * carry
            carry = jnp.broadcast_to(hv[SUBLANES - 1:SUBLANES], (SUBLANES, blk))
            tiles.append(hv)
        hh = jnp.concatenate(tiles, axis=0)
        h_ref[0:1, cs] = hh[tt - 1:tt, :]
        gt = gate_ref[:, cs]
        o_ref[:, cs] = (hh * (gt * _sigmoid(gt))).astype(o_ref.dtype)

    xbuf_ref[0:pad, :] = xbuf_ref[tt:tt + pad, :]


def _rglru(proj, x_col_block, gate_col_block, conv_w, conv_b, w_gates, b_a, b_x, lam, batch, tt, dc):
    M = proj.shape[0]
    T = M // batch
    d_rnn = conv_w.shape[1]
    cw = conv_w.shape[0]
    blk = w_gates.shape[1]
    tt, dc = min(tt, T), min(dc, d_rnn)
    assert T % tt == 0 and d_rnn % dc == 0 and dc % blk == 0 and cw - 1 <= 8 <= tt
    nt, nd = T // tt, d_rnn // dc
    vec = lambda: pl.BlockSpec((1, dc), lambda b, d, t: (0, d))
    return pl.pallas_call(
        functools.partial(_rglru_kernel, tt=tt, conv_w=cw), grid=(batch, nd, nt),
        in_specs=[pl.BlockSpec((tt, dc), lambda b, d, t: (b * nt + t, x_col_block * nd + d)),
                  pl.BlockSpec((tt, dc), lambda b, d, t: (b * nt + t, gate_col_block * nd + d)),
                  pl.BlockSpec((cw, dc), lambda b, d, t: (0, d)),
                  vec(),
                  pl.BlockSpec((dc // blk, blk, 2 * blk), lambda b, d, t: (d, 0, 0)),
                  vec(), vec(), vec()],
        out_specs=pl.BlockSpec((tt, dc), lambda b, d, t: (b * nt + t, d)),
        out_shape=jax.ShapeDtypeStruct((M, d_rnn), BF16),
        scratch_shapes=[pltpu.VMEM((tt + 8, dc), F32), pltpu.VMEM((8, dc), F32)],
        compiler_params=_params(3), name="rglru")(
            proj, proj, conv_w, conv_b, w_gates, b_a, b_x, lam)


def _merge_kernel(a_ref, hb_ref, wa_ref, wb_ref, ga_ref, gb_ref, o_ref):
    br_a = jnp.dot(a_ref[...], wa_ref[...], preferred_element_type=F32)
    br_b = jnp.dot(hb_ref[...], wb_ref[...], preferred_element_type=F32)
    o_ref[...] = (_sigmoid(ga_ref[...]) * br_a
                  + _sigmoid(gb_ref[...]) * br_b).astype(o_ref.dtype)


def _merge(a, hb, wa, wb, proj, ga_col_block, gb_col_block, tm, tn):
    M = a.shape[0]
    d_model = wa.shape[1]
    tm, tn = min(tm, M), min(tn, d_model)
    assert M % tm == 0 and d_model % tn == 0
    nn = d_model // tn
    return pl.pallas_call(
        _merge_kernel, grid=(M // tm, nn),
        in_specs=[pl.BlockSpec((tm, a.shape[1]), lambda i, j: (i, 0)),
                  pl.BlockSpec((tm, hb.shape[1]), lambda i, j: (i, 0)),
                  pl.BlockSpec((wa.shape[0], tn), lambda i, j: (0, j)),
                  pl.BlockSpec((wb.shape[0], tn), lambda i, j: (0, j)),
                  pl.BlockSpec((tm, tn), lambda i, j: (i, ga_col_block * nn + j)),
                  pl.BlockSpec((tm, tn), lambda i, j: (i, gb_col_block * nn + j))],
        out_specs=pl.BlockSpec((tm, tn), lambda i, j: (i, j)),
        out_shape=jax.ShapeDtypeStruct((M, d_model), BF16),
        compiler_params=_params(2), name="branch_merge")(a, hb, wa, wb, proj, proj)


def _out_kernel(m_ref, w_ref, x_ref, g_ref, b_ref, o_ref, *, alpha):
    sub = jnp.dot(m_ref[...], w_ref[...], preferred_element_type=F32)
    y = alpha * x_ref[...] + sub
    mu = jnp.mean(y, axis=-1, keepdims=True)
    var = jnp.mean(jnp.square(y - mu), axis=-1, keepdims=True)
    o_ref[...] = (y - mu) * lax.rsqrt(var + LN_EPS) * g_ref[...] + b_ref[...]


def _out_norm(merged, w_out, x, g, b, alpha, tm):
    M, d = x.shape
    tm = min(tm, M)
    assert M % tm == 0
    return pl.pallas_call(
        functools.partial(_out_kernel, alpha=alpha), grid=(M // tm,),
        in_specs=[pl.BlockSpec((tm, d), lambda i: (i, 0)),
                  pl.BlockSpec((d, d), lambda i: (0, 0)),
                  pl.BlockSpec((tm, d), lambda i: (i, 0)),
                  pl.BlockSpec((1, d), lambda i: (0, 0)),
                  pl.BlockSpec((1, d), lambda i: (0, 0))],
        out_specs=pl.BlockSpec((tm, d), lambda i: (i, 0)),
        out_shape=jax.ShapeDtypeStruct((M, d), F32),
        compiler_params=_params(1), name="out_proj_layernorm")(merged, w_out, x, g, b)


def _layer(x2, batch, w_in, kv_norm_g, w_uv, w_branch_a, conv_w, conv_b, w_gate_a, b_gate_a,
           w_gate_x, b_gate_x, lru_lambda, w_branch_b, rel_bias, w_out, ln_g, ln_b, alpha):
    M, d_model = x2.shape
    T = M // batch
    n_heads, d_lat, d_head = w_uv.shape
    d_attn = n_heads * d_head
    d_rnn = conv_w.shape[1]
    hi, di = N_HEADS_IDX, D_IDX
    sizes = (n_heads * d_lat, d_lat, d_attn, hi * di, di, hi, d_rnn, d_rnn, d_model, d_model)
    assert sum(sizes) == w_in.shape[1]
    offs = [0]
    for s in sizes:
        offs.append(offs[-1] + s)
    seg = lambda k: w_in[:, offs[k]:offs[k + 1]]
    assert d_attn == d_rnn == d_model and d_model % LANES == 0
    tq = TOPK_MAX
    topk = min(TOPK_MAX, T // 4)
    assert T % tq == 0 and (n_heads * d_lat) % (hi * di) == 0

    xb = x2.astype(BF16)
    w_rows = jnp.concatenate([seg(2), seg(6), seg(7), seg(8), seg(9)], axis=1).astype(BF16)
    proj = _matmul(xb, w_rows, 1024, 1024, F32)
    small_w = d_lat + di
    small_pad = -small_w % LANES
    w_small = jnp.pad(jnp.concatenate([seg(1), seg(4)], axis=1), ((0, 0), (0, small_pad))).astype(BF16)
    proj_small = _matmul(xb, w_small, 1024, small_w + small_pad, F32)
    att_scale = d_lat ** -0.5 * LOG2E
    idx_scale = (di ** -0.5) * (hi ** -0.5)
    w_qt = jnp.concatenate([seg(0) * att_scale, seg(3)], axis=1).T.astype(BF16)
    qt = _matmul_nt(w_qt, xb, batch, 512, 1024, BF16, "proj_queries_t")
    w_t = _matmul_nt((seg(5) * idx_scale).T.astype(BF16), xb, batch, hi, 1024, F32, "proj_idx_w_t")

    c, ct, k_idx = _kvnorm(proj_small, kv_norm_g.reshape(1, d_lat), batch, tq, d_lat, di)
    sel = _topk_select(qt, w_t, k_idx, tq, (n_heads * d_lat) // (hi * di), topk)
    bias = _bias_tiles(rel_bias, tq)
    a = _attention(rel_bias, qt, c, ct, sel, bias, w_uv.astype(BF16), proj, 0, tq, n_heads)

    w_gates = jnp.concatenate([w_gate_a, w_gate_x], axis=2).astype(BF16)
    row = lambda v: v.reshape(1, -1)
    hb = _rglru(proj, 1, 2, conv_w, row(conv_b), w_gates, row(b_gate_a), row(b_gate_x),
                row(lru_lambda), batch, 256, 512)

    merged = _merge(a, hb, w_branch_a.astype(BF16), w_branch_b.astype(BF16), proj, 3, 4, 1024, 512)
    return _out_norm(merged, w_out.astype(BF16), x2, row(ln_g), row(ln_b), alpha, 512)


def kernel(x, w_in, kv_norm_g, w_uv, w_branch_a, conv_w, conv_b, w_gate_a, b_gate_a, w_gate_x,
           b_gate_x, lru_lambda, w_branch_b, rel_bias, w_out, ln_g, ln_b):
    batch, T, d_model = x.shape
    depth = w_in.shape[0]
    alpha = (2 * depth) ** 0.25
    x2 = x.reshape(batch * T, d_model)
    for l in range(depth):
        x2 = _layer(x2, batch, w_in[l], kv_norm_g[l], w_uv[l], w_branch_a[l], conv_w[l], conv_b[l],
                    w_gate_a[l], b_gate_a[l], w_gate_x[l], b_gate_x[l], lru_lambda[l], w_branch_b[l],
                    rel_bias, w_out[l], ln_g[l], ln_b[l], alpha)
    return x2.reshape(batch, T, d_model)
```

```python
import functools
import math

import jax
import jax.numpy as jnp
from jax import lax
from jax.experimental import pallas as pl
from jax.experimental.pallas import tpu as pltpu

N_HEADS_IDX = 16
D_IDX = 64
TOPK_MAX = 256
MAX_DIST = 128
LRU_C = 8.0
LN_EPS = 1e-5

LANES = 128
SUBLANES = 8
V7X_VMEM_LIMIT_BYTES = 56 * 1024 * 1024

NEG = -1e30
INT_MIN = -(2 ** 31)
LOG2E = math.log2(math.e)

F32 = jnp.float32
BF16 = jnp.bfloat16


def _sigmoid(x):
    return 0.5 * jnp.tanh(0.5 * x) + 0.5


def _params(n_grid):
    return pltpu.CompilerParams(dimension_semantics=("arbitrary",) * n_grid,
                                vmem_limit_bytes=V7X_VMEM_LIMIT_BYTES)


def _mm_kernel(x_ref, w_ref, o_ref):
    o_ref[...] = jnp.dot(x_ref[...], w_ref[...], preferred_element_type=F32).astype(o_ref.dtype)


def _matmul(x, w, tm, tn, out_dtype):
    M, K = x.shape
    N = w.shape[1]
    tm, tn = min(tm, M), min(tn, N)
    assert M % tm == 0 and N % tn == 0
    return pl.pallas_call(
        _mm_kernel, grid=(M // tm, N // tn),
        in_specs=[pl.BlockSpec((tm, K), lambda i, j: (i, 0)),
                  pl.BlockSpec((K, tn), lambda i, j: (0, j))],
        out_specs=pl.BlockSpec((tm, tn), lambda i, j: (i, j)),
        out_shape=jax.ShapeDtypeStruct((M, N), out_dtype),
        compiler_params=_params(2), name="proj_rows")(x, w)


def _mm_nt_kernel(scale_ref, w_ref, ww_ref, x_ref, o_ref, wt_ref, *, w_scale):
    n = pl.program_id(2)
    nt_dims = (((1,), (1,)), ((), ()))
    acc = lax.dot_general(w_ref[...], x_ref[...], nt_dims, preferred_element_type=F32)
    o_ref[0] = (acc * scale_ref[n]).astype(o_ref.dtype)

    @pl.when(n == 0)
    def _():
        wt_ref[0] = lax.dot_general(ww_ref[...], x_ref[...], nt_dims, preferred_element_type=F32) * w_scale


def _proj_queries_t(w_t, scales, ww_t, w_scale, x, batch, tn, tt):
    N, K = w_t.shape
    hi = ww_t.shape[0]
    M = x.shape[0]
    T = M // batch
    tt = min(tt, T)
    assert N % tn == 0 and T % tt == 0 and scales.shape == (N // tn,)
    nt = T // tt
    return pl.pallas_call(
        functools.partial(_mm_nt_kernel, w_scale=w_scale), grid=(batch, nt, N // tn),
        in_specs=[pl.BlockSpec(memory_space=pltpu.SMEM),
                  pl.BlockSpec((tn, K), lambda b, t, n: (n, 0)),
                  pl.BlockSpec((hi, K), lambda b, t, n: (0, 0)),
                  pl.BlockSpec((tt, K), lambda b, t, n: (b * nt + t, 0))],
        out_specs=[pl.BlockSpec((1, tn, tt), lambda b, t, n: (b, n, t)),
                   pl.BlockSpec((1, hi, tt), lambda b, t, n: (b, 0, t))],
        out_shape=[jax.ShapeDtypeStruct((batch, N, T), BF16),
                   jax.ShapeDtypeStruct((batch, hi, T), F32)],
        compiler_params=_params(3), name="proj_queries_t")(scales, w_t, ww_t, x)


def _kv_kernel(x_ref, w_ref, g_ref, c_ref, ct_ref, k_ref, *, d_lat, d_idx, tk):
    p = jnp.dot(x_ref[...], w_ref[...], preferred_element_type=F32)
    xf = p[:, :d_lat]
    c = xf * lax.rsqrt(jnp.mean(jnp.square(xf), axis=-1, keepdims=True) + LN_EPS) * g_ref[...]
    c_ref[0] = c.astype(BF16)
    for u in range(x_ref.shape[0] // tk):
        ct_ref[0, u] = c[u * tk:(u + 1) * tk].T.astype(BF16)
    k_ref[0] = p[:, d_lat:d_lat + d_idx].astype(BF16)


def _kv_proj(x, w_small, g, batch, tm, tk, d_lat, d_idx):
    M, K = x.shape
    W = w_small.shape[1]
    T = M // batch
    tm = min(tm, T)
    assert T % tm == 0 and tm % tk == 0
    nt = T // tm
    return pl.pallas_call(
        functools.partial(_kv_kernel, d_lat=d_lat, d_idx=d_idx, tk=tk), grid=(batch, nt),
        in_specs=[pl.BlockSpec((tm, K), lambda b, t: (b * nt + t, 0)),
                  pl.BlockSpec((K, W), lambda b, t: (0, 0)),
                  pl.BlockSpec((1, d_lat), lambda b, t: (0, 0))],
        out_specs=[pl.BlockSpec((1, tm, d_lat), lambda b, t: (b, t, 0)),
                   pl.BlockSpec((1, tm // tk, d_lat, tk), lambda b, t: (b, t, 0, 0)),
                   pl.BlockSpec((1, tm, d_idx), lambda b, t: (b, t, 0))],
        out_shape=[jax.ShapeDtypeStruct((batch, T, d_lat), BF16),
                   jax.ShapeDtypeStruct((batch, T // tk, d_lat, tk), BF16),
                   jax.ShapeDtypeStruct((batch, T, d_idx), BF16)],
        compiler_params=_params(2), name="proj_kv_norm")(x, w_small, g)


PACK = 32
PACK_ROWS = 8 * PACK
_XPOSE_MASKS = {16: 0x0000FFFF, 8: 0x00FF00FF, 4: 0x0F0F0F0F, 2: 0x33333333, 1: 0x55555555}


def _bit_transpose(words):
    a = list(words)
    for j in (16, 8, 4, 2, 1):
        m = jnp.int32(_XPOSE_MASKS[j])
        for k in range(PACK):
            if k & j == 0:
                lo, hi = a[k], a[k + j]
                t = (lax.shift_right_logical(lo, jnp.int32(j)) ^ hi) & m
                a[k] = lo ^ (t << j)
                a[k + j] = hi ^ t
    return a


def _rows_below(limit, base):
    nk = jnp.clip((limit - base + 7) >> 3, 0, PACK)
    return jnp.where(nk >= PACK, -1, (jnp.int32(1) << jnp.minimum(nk, PACK - 1)) - 1)


def _topk_kernel(qi_ref, w_ref, k_ref, sel_ref, keys_ref, planes_ref, alive_ref, greater_ref,
                 *, tq, tk, topk):
    b = pl.program_id(0)
    i = pl.program_id(1)
    T = k_ref.shape[1]
    hi = w_ref.shape[1]
    di = k_ref.shape[2]
    R = T // PACK
    n_chunks = (i + 1) * tq // PACK_ROWS
    q_pos = i * tq + lax.broadcasted_iota(jnp.int32, (1, tq), 1)

    @pl.when((b == 0) & (i == 0))
    def _():
        planes_ref[...] = jnp.zeros(planes_ref.shape, jnp.int32)

    def chunk(c, slot):
        k0 = slot * PACK_ROWS
        for half in range(PACK_ROWS // tk):
            r0 = pl.multiple_of(c * PACK_ROWS + half * tk, tk)
            kc = k_ref[0, pl.ds(r0, tk), :]
            acc = jnp.zeros((tk, tq), F32)
            for h in range(hi):
                s = jnp.dot(kc, qi_ref[0, h * di:(h + 1) * di, :], preferred_element_type=F32)
                acc = acc + w_ref[0, h:h + 1, :] * jnp.maximum(s, 0.0)
            bits = pltpu.bitcast(acc, jnp.int32)
            ukey = bits ^ ((bits >> 31) | INT_MIN)
            k_pos = r0 + lax.broadcasted_iota(jnp.int32, (tk, tq), 0)
            keys_ref[k0 + half * tk:k0 + (half + 1) * tk, :] = jnp.where(k_pos <= q_pos, ukey, 0)
        p0 = pl.multiple_of(c * 8, 8)
        for lt in range(tq // LANES):
            ls = slice(lt * LANES, (lt + 1) * LANES)
            planes = _bit_transpose([keys_ref[k0 + 8 * k:k0 + 8 * k + 8, ls] for k in range(PACK)])
            for bit in range(PACK):
                planes_ref[bit, pl.ds(p0, 8), ls] = planes[bit]

    def chunk_pair(n, carry):
        chunk(2 * n, 0)
        chunk(2 * n + 1, 1)
        return carry

    def chunk_single(c, carry):
        chunk(c, 0)
        return carry

    lax.fori_loop(0, n_chunks // 2, chunk_pair, 0)
    lax.fori_loop(2 * (n_chunks // 2), n_chunks, chunk_single, 0)

    prow = lax.broadcasted_iota(jnp.int32, (R, tq), 0)
    base = (prow >> 3) * PACK_ROWS + (prow & 7)
    alive_ref[...] = jnp.where((prow >> 3) < n_chunks, -1, 0)
    greater_ref[...] = jnp.zeros((R, tq), jnp.int32)

    def popcount_rows(words):
        return jnp.sum(lax.population_count(words), axis=0, keepdims=True)

    def select_bit(n, n_greater):
        plane = planes_ref[PACK - 1 - n]
        alive = alive_ref[...]
        ones = alive & plane
        cnt = popcount_rows(ones)
        take = (n_greater + cnt) >= topk
        alive_ref[...] = jnp.where(take, ones, alive & ~plane)
        greater_ref[...] = jnp.where(take, greater_ref[...], greater_ref[...] | ones)
        return jnp.where(take, n_greater, n_greater + cnt)

    n_greater = lax.fori_loop(0, PACK, select_bit, jnp.zeros((1, tq), jnp.int32))
    need = topk - n_greater
    n_tie = popcount_rows(alive_ref[...])

    sel_ref[0] = greater_ref[...] | alive_ref[...]

    @pl.when(jnp.max(n_tie - need) > 0)
    def _():
        nbits = (T - 1).bit_length()
        ties = alive_ref[...]

        def jbody(n, jc):
            cand = jc | jnp.left_shift(jnp.int32(1), nbits - 1 - n)
            f = popcount_rows(ties & _rows_below(cand, base))
            return jnp.where(f < need, cand, jc)

        jc = lax.fori_loop(0, nbits, jbody, jnp.zeros((1, tq), jnp.int32))
        sel_ref[0] = greater_ref[...] | (ties & _rows_below(jc + 1, base))

    sel_ref[0] = sel_ref[0] & _rows_below(q_pos + 1, base)


def _topk_select(qt, w_t, k_idx, tq, qi_row_block, topk):
    batch, hi, T = w_t.shape
    di = k_idx.shape[2]
    tk = 128
    assert T % tq == 0 and tq % PACK_ROWS == 0 and PACK_ROWS % tk == 0 and tq % LANES == 0
    R = T // PACK
    return pl.pallas_call(
        functools.partial(_topk_kernel, tq=tq, tk=tk, topk=topk),
        grid=(batch, T // tq),
        in_specs=[pl.BlockSpec((1, hi * di, tq), lambda b, i: (b, qi_row_block, i)),
                  pl.BlockSpec((1, hi, tq), lambda b, i: (b, 0, i)),
                  pl.BlockSpec((1, T, di), lambda b, i: (b, 0, 0))],
        out_specs=pl.BlockSpec((1, R, tq), lambda b, i: (b, 0, i)),
        out_shape=jax.ShapeDtypeStruct((batch, R, T), jnp.int32),
        scratch_shapes=[pltpu.VMEM((2 * PACK_ROWS, tq), jnp.int32), pltpu.VMEM((PACK, R, tq), jnp.int32),
                        pltpu.VMEM((R, tq), jnp.int32), pltpu.VMEM((R, tq), jnp.int32)],
        compiler_params=_params(2), name="indexer_topk")(qt, w_t, k_idx)


def _bias_kernel(rel_ref, o_ref, *, tq, n_buckets):
    h = pl.program_id(0)
    max_exact = n_buckets // 2
    kk = lax.broadcasted_iota(jnp.int32, (tq, tq), 0)
    qq = lax.broadcasted_iota(jnp.int32, (tq, tq), 1)
    for r in range(2):
        n = jnp.maximum(r * tq + qq - kk, 0)
        nf = jnp.maximum(n, 1).astype(F32)
        large = max_exact + (jnp.log(nf / max_exact) / math.log(MAX_DIST / max_exact)
                             * (n_buckets - max_exact)).astype(jnp.int32)
        bucket = jnp.where(n < max_exact, n, jnp.minimum(large, n_buckets - 1))
        tile = jnp.zeros((tq, tq), F32)
        for k in range(n_buckets):
            tile = jnp.where(bucket == k, rel_ref[k, h], tile)
        o_ref[0, r] = tile * LOG2E


def _bias_tiles(rel_bias, tq):
    n_buckets, n_heads = rel_bias.shape
    return pl.pallas_call(
        functools.partial(_bias_kernel, tq=tq, n_buckets=n_buckets), grid=(n_heads,),
        in_specs=[pl.BlockSpec(memory_space=pltpu.SMEM)],
        out_specs=pl.BlockSpec((1, 2, tq, tq), lambda h: (h, 0, 0, 0)),
        out_shape=jax.ShapeDtypeStruct((n_heads, 2, tq, tq), F32),
        compiler_params=_params(1), name="rel_bias_tiles")(rel_bias)


def _attn_kernel(rel_ref, q_ref, c_ref, ct_ref, sel_ref, bias_ref, wuv_ref, gate_ref, o_ref,
                 m_ref, l_ref, acc_ref, *, tq, n_group):
    i = pl.program_id(1)
    g = pl.program_id(2)
    d_lat = c_ref.shape[2]
    d_head = wuv_ref.shape[2]
    m_ref[...] = jnp.full(m_ref.shape, NEG, F32)
    l_ref[...] = jnp.zeros(l_ref.shape, F32)
    acc_ref[...] = jnp.zeros(acc_ref.shape, F32)

    def step(j, bias_slots, n_sub):
        r0 = pl.multiple_of(j * tq, tq)
        cj = c_ref[0, pl.ds(r0, n_sub * tq), :]
        ctj = jnp.concatenate([ct_ref[0, j + u] for u in range(n_sub)], axis=1)
        words = sel_ref[0, pl.ds(pl.multiple_of(j * 8, 8), n_sub * 8), :]
        mk = jnp.concatenate([jnp.where((words[8 * u:8 * u + 8] << (PACK - 1 - k)) < 0, 0.0, NEG)
                              for u in range(n_sub) for k in range(PACK)], axis=0)
        for hh in range(n_group):
            s = jnp.dot(cj, q_ref[0, hh * d_lat:(hh + 1) * d_lat, :], preferred_element_type=F32) + mk
            if bias_slots is not None:
                s = s + jnp.concatenate([bias_ref[hh, slot] for slot in bias_slots], axis=0)
            m_old = m_ref[hh, 0:1, :]
            m_new = jnp.maximum(m_old, jnp.max(s, axis=0, keepdims=True))
            alpha = jnp.exp2(m_old - m_new)
            p = jnp.exp2(s - m_new)
            l_ref[hh, 0:1, :] = alpha * l_ref[hh, 0:1, :] + jnp.sum(p, axis=0, keepdims=True)
            m_ref[hh, 0:1, :] = m_new
            acc_ref[hh] = alpha * acc_ref[hh] + jnp.dot(ctj, p.astype(BF16), preferred_element_type=F32)

    n_far = jnp.maximum(i - 1, 0)
    done = 0
    for n_sub in (2, 1):
        trips = (n_far - done) // n_sub

        def far(k, carry, n_sub=n_sub, done=done):
            step(done + n_sub * k, None, n_sub)
            return carry

        lax.fori_loop(0, trips, far, 0)
        done = done + n_sub * trips
    for hh in range(n_group):
        m_ref[hh, 0:1, :] = m_ref[hh, 0:1, :] + rel_ref[rel_ref.shape[0] - 1, g * n_group + hh] * LOG2E

    @pl.when(i >= 1)
    def _():
        step(i - 1, (1, 0), 2)

    @pl.when(i == 0)
    def _():
        step(0, (0,), 1)

    for hh in range(n_group):
        o = acc_ref[hh] * (1.0 / l_ref[hh, 0:1, :])
        val = jnp.dot(o.T.astype(BF16), wuv_ref[hh], preferred_element_type=F32)
        gt = gate_ref[:, hh * d_head:(hh + 1) * d_head]
        o_ref[:, hh * d_head:(hh + 1) * d_head] = (val * (gt * _sigmoid(gt))).astype(o_ref.dtype)


def _attention(rel_bias, qt, c, ct, sel, bias, w_uv, proj, gate_col_block, tq, n_group):
    batch, T, d_lat = c.shape
    n_heads, _, d_head = w_uv.shape
    nq = T // tq
    ng = n_heads // n_group
    assert tq >= MAX_DIST and tq == PACK_ROWS and d_head % LANES == 0 and n_heads % n_group == 0
    once = pl.Buffered(1)
    return pl.pallas_call(
        functools.partial(_attn_kernel, tq=tq, n_group=n_group), grid=(batch, nq, ng),
        in_specs=[pl.BlockSpec(memory_space=pltpu.SMEM),
                  pl.BlockSpec((1, n_group * d_lat, tq), lambda b, i, g: (b, g, i)),
                  pl.BlockSpec((1, T, d_lat), lambda b, i, g: (b, 0, 0), pipeline_mode=once),
                  pl.BlockSpec((1, nq, d_lat, tq), lambda b, i, g: (b, 0, 0, 0), pipeline_mode=once),
                  pl.BlockSpec((1, T // PACK, tq), lambda b, i, g: (b, 0, i)),
                  pl.BlockSpec((n_group, 2, tq, tq), lambda b, i, g: (g, 0, 0, 0), pipeline_mode=once),
                  pl.BlockSpec((n_group, d_lat, d_head), lambda b, i, g: (g, 0, 0), pipeline_mode=once),
                  pl.BlockSpec((tq, n_group * d_head),
                               lambda b, i, g: (b * nq + i, gate_col_block * ng + g))],
        out_specs=pl.BlockSpec((tq, n_group * d_head), lambda b, i, g: (b * nq + i, g)),
        out_shape=jax.ShapeDtypeStruct((batch * T, n_heads * d_head), BF16),
        scratch_shapes=[pltpu.VMEM((n_group, SUBLANES, tq), F32), pltpu.VMEM((n_group, SUBLANES, tq), F32),
                        pltpu.VMEM((n_group, d_lat, tq), F32)],
        compiler_params=_params(3), name="sparse_attention")(
            rel_bias, qt, c, ct, sel, bias, w_uv, proj)


def _rglru_kernel(x_ref, gate_ref, cw_ref, cb_ref, wg_ref, ba_ref, bx_ref, lam_ref, o_ref,
                  xbuf_ref, h_ref, *, tt, conv_w):
    t = pl.program_id(2)
    dc = x_ref.shape[1]
    blk = wg_ref.shape[1]
    pad = 8

    @pl.when(t == 0)
    def _():
        xbuf_ref[0:pad, :] = jnp.zeros((pad, dc), F32)
        h_ref[...] = jnp.zeros(h_ref.shape, F32)

    xbuf_ref[pad:pad + tt, :] = x_ref[...]
    row = lax.broadcasted_iota(jnp.int32, (tt, blk), 0)
    first = (row + t * tt) == 0
    sub = lax.broadcasted_iota(jnp.int32, (SUBLANES, blk), 0)

    for n in range(dc // blk):
        cs = slice(n * blk, (n + 1) * blk)
        xc = cb_ref[:, cs]
        for k in range(conv_w):
            off = pad - (conv_w - 1) + k
            xc = xc + cw_ref[k:k + 1, cs] * xbuf_ref[off:off + tt, cs]
        g = jnp.dot(xc.astype(BF16), wg_ref[n], preferred_element_type=F32)
        r_gate = _sigmoid(g[:, :blk] + ba_ref[:, cs])
        i_gate = _sigmoid(g[:, blk:] + bx_ref[:, cs])
        lam = lam_ref[:, cs]
        softplus_neg = jnp.maximum(-lam, 0.0) + jnp.log(1.0 + jnp.exp(-jnp.abs(lam)))
        a = jnp.exp2(r_gate * (softplus_neg * (-LRU_C * LOG2E)))
        mult = jnp.where(first, 1.0, jnp.sqrt(1.0 - a * a))
        bv = mult * (i_gate * xc)
        carry = jnp.broadcast_to(h_ref[0:1, cs], (SUBLANES, blk))
        tiles = []
        for v in range(tt // SUBLANES):
            av = a[v * SUBLANES:(v + 1) * SUBLANES]
            bw = bv[v * SUBLANES:(v + 1) * SUBLANES]
            s = 1
            while s < SUBLANES:
                valid = sub >= s
                bw = jnp.where(valid, av * pltpu.roll(bw, s, 0) + bw, bw)
                av = jnp.where(valid, av * pltpu.roll(av, s, 0), av)
                s *= 2
            hv = bw + av * carry
            carry = jnp.broadcast_to(hv[SUBLANES - 1:SUBLANES], (SUBLANES, blk))
            tiles.append(hv)
        hh = jnp.concatenate(tiles, axis=0)
        h_ref[0:1, cs] = hh[tt - 1:tt, :]
        gt = gate_ref[:, cs]
        o_ref[:, cs] = (hh * (gt * _sigmoid(gt))).astype(o_ref.dtype)

    xbuf_ref[0:pad, :] = xbuf_ref[tt:tt + pad, :]


def _rglru(proj, x_col_block, gate_col_block, conv_w, conv_b, w_gates, b_a, b_x, lam, batch, tt, dc):
    M = proj.shape[0]
    T = M // batch
    d_rnn = conv_w.shape[1]
    cw = conv_w.shape[0]
    blk = w_gates.shape[1]
    tt, dc = min(tt, T), min(dc, d_rnn)
    assert T % tt == 0 and d_rnn % dc == 0 and dc % blk == 0 and cw - 1 <= 8 <= tt
    nt, nd = T // tt, d_rnn // dc
    vec = lambda: pl.BlockSpec((1, dc), lambda b, d, t: (0, d))
    return pl.pallas_call(
        functools.partial(_rglru_kernel, tt=tt, conv_w=cw), grid=(batch, nd, nt),
        in_specs=[pl.BlockSpec((tt, dc), lambda b, d, t: (b * nt + t, x_col_block * nd + d)),
                  pl.BlockSpec((tt, dc), lambda b, d, t: (b * nt + t, gate_col_block * nd + d)),
                  pl.BlockSpec((cw, dc), lambda b, d, t: (0, d)),
                  vec(),
                  pl.BlockSpec((dc // blk, blk, 2 * blk), lambda b, d, t: (d, 0, 0)),
                  vec(), vec(), vec()],
        out_specs=pl.BlockSpec((tt, dc), lambda b, d, t: (b * nt + t, d)),
        out_shape=jax.ShapeDtypeStruct((M, d_rnn), BF16),
        scratch_shapes=[pltpu.VMEM((tt + 8, dc), F32), pltpu.VMEM((8, dc), F32)],
        compiler_params=_params(3), name="rglru")(
            proj, proj, conv_w, conv_b, w_gates, b_a, b_x, lam)


def _merge_kernel(a_ref, hb_ref, wa_ref, wb_ref, ga_ref, gb_ref, o_ref):
    br_a = jnp.dot(a_ref[...], wa_ref[...], preferred_element_type=F32)
    br_b = jnp.dot(hb_ref[...], wb_ref[...], preferred_element_type=F32)
    o_ref[...] = (_sigmoid(ga_ref[...]) * br_a
                  + _sigmoid(gb_ref[...]) * br_b).astype(o_ref.dtype)


def _merge(a, hb, wa, wb, proj, ga_col_block, gb_col_block, tm, tn):
    M = a.shape[0]
    d_model = wa.shape[1]
    tm, tn = min(tm, M), min(tn, d_model)
    assert M % tm == 0 and d_model % tn == 0
    nn = d_model // tn
    return pl.pallas_call(
        _merge_kernel, grid=(M // tm, nn),
        in_specs=[pl.BlockSpec((tm, a.shape[1]), lambda i, j: (i, 0)),
                  pl.BlockSpec((tm, hb.shape[1]), lambda i, j: (i, 0)),
                  pl.BlockSpec((wa.shape[0], tn), lambda i, j: (0, j)),
                  pl.BlockSpec((wb.shape[0], tn), lambda i, j: (0, j)),
                  pl.BlockSpec((tm, tn), lambda i, j: (i, ga_col_block * nn + j)),
                  pl.BlockSpec((tm, tn), lambda i, j: (i, gb_col_block * nn + j))],
        out_specs=pl.BlockSpec((tm, tn), lambda i, j: (i, j)),
        out_shape=jax.ShapeDtypeStruct((M, d_model), BF16),
        compiler_params=_params(2), name="branch_merge")(a, hb, wa, wb, proj, proj)


def _out_kernel(m_ref, w_ref, x_ref, g_ref, b_ref, o_ref, *, alpha):
    sub = jnp.dot(m_ref[...], w_ref[...], preferred_element_type=F32)
    y = alpha * x_ref[...] + sub
    mu = jnp.mean(y, axis=-1, keepdims=True)
    var = jnp.mean(jnp.square(y - mu), axis=-1, keepdims=True)
    o_ref[...] = (y - mu) * lax.rsqrt(var + LN_EPS) * g_ref[...] + b_ref[...]


def _out_norm(merged, w_out, x, g, b, alpha, tm):
    M, d = x.shape
    tm = min(tm, M)
    assert M % tm == 0
    return pl.pallas_call(
        functools.partial(_out_kernel, alpha=alpha), grid=(M // tm,),
        in_specs=[pl.BlockSpec((tm, d), lambda i: (i, 0)),
                  pl.BlockSpec((d, d), lambda i: (0, 0)),
                  pl.BlockSpec((tm, d), lambda i: (i, 0)),
                  pl.BlockSpec((1, d), lambda i: (0, 0)),
                  pl.BlockSpec((1, d), lambda i: (0, 0))],
        out_specs=pl.BlockSpec((tm, d), lambda i: (i, 0)),
        out_shape=jax.ShapeDtypeStruct((M, d), F32),
        compiler_params=_params(1), name="out_proj_layernorm")(merged, w_out, x, g, b)


def _layer(x2, batch, w_in, kv_norm_g, w_uv, w_branch_a, conv_w, conv_b, w_gate_a, b_gate_a,
           w_gate_x, b_gate_x, lru_lambda, w_branch_b, rel_bias, w_out, ln_g, ln_b, alpha):
    M, d_model = x2.shape
    T = M // batch
    n_heads, d_lat, d_head = w_uv.shape
    d_attn = n_heads * d_head
    d_rnn = conv_w.shape[1]
    hi, di = N_HEADS_IDX, D_IDX
    sizes = (n_heads * d_lat, d_lat, d_attn, hi * di, di, hi, d_rnn, d_rnn, d_model, d_model)
    assert sum(sizes) == w_in.shape[1]
    offs = [0]
    for s in sizes:
        offs.append(offs[-1] + s)
    w_bf = w_in.astype(BF16)
    seg = lambda k: w_bf[:, offs[k]:offs[k + 1]]
    assert d_attn == d_rnn == d_model and d_model % LANES == 0
    tq = TOPK_MAX
    topk = min(TOPK_MAX, T // 4)
    tn_q = hi * di
    assert T % tq == 0 and (n_heads * d_lat) % tn_q == 0

    xb = x2.astype(BF16)
    w_rows = jnp.concatenate([seg(2), seg(6), seg(7), seg(8), seg(9)], axis=1)
    proj = _matmul(xb, w_rows, 1024, 1024, F32)
    small_pad = -(d_lat + di) % LANES
    w_small = jnp.pad(jnp.concatenate([seg(1), seg(4)], axis=1), ((0, 0), (0, small_pad)))
    c, ct, k_idx = _kv_proj(xb, w_small, kv_norm_g.reshape(1, d_lat), batch, 1024, tq, d_lat, di)
    att_scale = d_lat ** -0.5 * LOG2E
    idx_scale = (di ** -0.5) * (hi ** -0.5)
    n_lat_blocks = n_heads * d_lat // tn_q
    scales = jnp.asarray([att_scale] * n_lat_blocks + [1.0], F32)
    w_qt = jnp.concatenate([seg(0), seg(3)], axis=1).T
    qt, w_t = _proj_queries_t(w_qt, scales, seg(5).T, idx_scale, xb, batch, tn_q, 1024)

    sel = _topk_select(qt, w_t, k_idx, tq, (n_heads * d_lat) // (hi * di), topk)
    bias = _bias_tiles(rel_bias, tq)
    a = _attention(rel_bias, qt, c, ct, sel, bias, w_uv.astype(BF16), proj, 0, tq, n_heads)

    w_gates = jnp.concatenate([w_gate_a, w_gate_x], axis=2).astype(BF16)
    row = lambda v: v.reshape(1, -1)
    hb = _rglru(proj, 1, 2, conv_w, row(conv_b), w_gates, row(b_gate_a), row(b_gate_x),
                row(lru_lambda), batch, 256, 512)

    merged = _merge(a, hb, w_branch_a.astype(BF16), w_branch_b.astype(BF16), proj, 3, 4, 1024, 512)
    return _out_norm(merged, w_out.astype(BF16), x2, row(ln_g), row(ln_b), alpha, 512)


def kernel(x, w_in, kv_norm_g, w_uv, w_branch_a, conv_w, conv_b, w_gate_a, b_gate_a, w_gate_x,
           b_gate_x, lru_lambda, w_branch_b, rel_bias, w_out, ln_g, ln_b):
    batch, T, d_model = x.shape
    depth = w_in.shape[0]
    alpha = (2 * depth) ** 0.25
    x2 = x.reshape(batch * T, d_model)
    for l in range(depth):
        x2 = _layer(x2, batch, w_in[l], kv_norm_g[l], w_uv[l], w_branch_a[l], conv_w[l], conv_b[l],
                    w_gate_a[l], b_gate_a[l], w_gate_x[l], b_gate_x[l], lru_lambda[l], w_branch_b[l],
                    rel_bias, w_out[l], ln_g[l], ln_b[l], alpha)
    return x2.reshape(batch, T, d_model)
```
